```python
import math
import jax, jax.numpy as jnp
from jax import lax
import numpy as np

D_MODEL = 1024
BATCH = 32
SEQ = 2048
DEPTH = 1

N_DIFF_HEADS = 4
DIFF_HEAD_DIM = 64
DIFF_V_DIM = 2 * DIFF_HEAD_DIM
DIFF_WIDTH = N_DIFF_HEADS * DIFF_V_DIM
N_MLSTM_HEADS = 4
MLSTM_QK_DIM = 64
MLSTM_V_DIM = 128
MLSTM_WIDTH = N_MLSTM_HEADS * MLSTM_V_DIM
MIX_WIDTH = DIFF_WIDTH + MLSTM_WIDTH
CONV_WIDTH = 4
CHUNK = 64
Q_BLOCK = 128
D_FF = 4 * D_MODEL
ROPE_THETA = 10000.0
EPS = 1e-6

IN_SIZES = (
    N_DIFF_HEADS * 2 * DIFF_HEAD_DIM,
    N_DIFF_HEADS * 2 * DIFF_HEAD_DIM,
    DIFF_WIDTH,
    N_MLSTM_HEADS * MLSTM_QK_DIM,
    N_MLSTM_HEADS * MLSTM_QK_DIM,
    MLSTM_WIDTH,
    MLSTM_WIDTH,
    N_MLSTM_HEADS,
    N_MLSTM_HEADS,
)
IN_WIDTH = sum(IN_SIZES)
IN_OFFSETS = tuple(int(o) for o in np.cumsum(IN_SIZES)[:-1])
CONV_CH = 2 * N_MLSTM_HEADS * MLSTM_QK_DIM

kernel_name = "hybrid_diffattn_mlstm_parallel_heads"


def lambda_init(layer):
    return 0.8 - 0.6 * math.exp(-0.3 * layer)


def rmsnorm(x, g):
    xf = x.astype(jnp.float32)
    y = xf * lax.rsqrt(jnp.mean(xf * xf, axis=-1, keepdims=True) + EPS)
    return (y * g.astype(jnp.float32)).astype(x.dtype)


def rope(x, pos):
    d = x.shape[-1]
    inv = ROPE_THETA ** (-jnp.arange(0, d, 2, dtype=jnp.float32) / d)
    ang = pos.astype(jnp.float32)[:, None] * inv[None, :]
    cos = jnp.cos(ang)[None, :, None, :]
    sin = jnp.sin(ang)[None, :, None, :]
    xf = x.astype(jnp.float32)
    x1, x2 = xf[..., : d // 2], xf[..., d // 2:]
    out = jnp.concatenate([x1 * cos - x2 * sin, x2 * cos + x1 * sin], axis=-1)
    return out.astype(x.dtype)


def diff_attention(q, k, v, lam):
    S = q.shape[3]
    scale = DIFF_HEAD_DIM ** -0.5
    outs = []
    for blk in range(S // Q_BLOCK):
        lo, hi = blk * Q_BLOCK, (blk + 1) * Q_BLOCK
        qb = q[:, :, :, lo:hi]
        kb = k[:, :, :, :hi]
        s = jnp.einsum('bhcqd,bhckd->bhcqk', qb, kb).astype(jnp.float32) * scale
        mask = (lo + jnp.arange(Q_BLOCK))[:, None] >= jnp.arange(hi)[None, :]
        s = jnp.where(mask, s, -jnp.inf)
        p = jax.nn.softmax(s, axis=-1)
        p_diff = p[:, :, 0] - lam * p[:, :, 1]
        outs.append(jnp.einsum('bhqk,bhkv->bhqv', p_diff.astype(v.dtype), v[:, :, :hi]))
    return jnp.concatenate(outs, axis=2)


def mlstm_chunkwise(q, k, v, i_pre, logf):
    B, H, S, dqk = q.shape
    dv = v.shape[-1]
    nc = S // CHUNK
    q = q * (dqk ** -0.5)

    def chunked(t):
        t = t.reshape(t.shape[:2] + (nc, CHUNK) + t.shape[3:])
        return jnp.moveaxis(t, 2, 0)

    xs = (chunked(q), chunked(k), chunked(v), chunked(i_pre), chunked(logf))
    causal = jnp.tril(jnp.ones((CHUNK, CHUNK), dtype=bool))

    def step(carry, inp):
        C, n, m = carry
        qc, kc, vc, ic, fc = inp
        b = jnp.cumsum(fc, axis=-1)
        g = b[..., -1]
        logD = b[..., :, None] - b[..., None, :] + ic[..., None, :]
        logD = jnp.where(causal, logD, -jnp.inf)
        inter = b + m[..., None]
        m_comb = jnp.maximum(inter, jnp.max(logD, axis=-1))
        scores = jnp.einsum('bhjd,bhsd->bhjs', qc, kc) * jnp.exp(logD - m_comb[..., None])
        w_inter = jnp.exp(inter - m_comb)
        num = (w_inter[..., None] * jnp.einsum('bhjd,bhdv->bhjv', qc, C)
               + jnp.einsum('bhjs,bhsv->bhjv', scores, vc))
        den = w_inter * jnp.einsum('bhjd,bhd->bhj', qc, n) + jnp.sum(scores, axis=-1)
        h = num / jnp.maximum(jnp.abs(den), jnp.exp(-m_comb))[..., None]
        log_w = g[..., None] - b + ic
        m_new = jnp.maximum(g + m, jnp.max(log_w, axis=-1))
        decay = jnp.exp(g + m - m_new)
        w = jnp.exp(log_w - m_new[..., None])
        C_new = decay[..., None, None] * C + jnp.einsum('bhs,bhsd,bhsv->bhdv', w, kc, vc)
        n_new = decay[..., None] * n + jnp.einsum('bhs,bhsd->bhd', w, kc)
        return (C_new, n_new, m_new), h

    init = (jnp.zeros((B, H, dqk, dv), jnp.float32),
            jnp.zeros((B, H, dqk), jnp.float32),
            jnp.zeros((B, H), jnp.float32))
    _, hs = lax.scan(step, init, xs)
    return jnp.moveaxis(hs, 0, 2).reshape(B, H, S, dv)


def causal_dwconv(x, w, b):
    S = x.shape[1]
    xp = jnp.pad(x, ((0, 0), (CONV_WIDTH - 1, 0), (0, 0)))
    y = b
    for j in range(CONV_WIDTH):
        y = y + w[j] * xp[:, j:j + S]
    return y


def setup_inputs(seed: int = 0) -> dict:
    key = jax.random.key(seed)
    ks = jax.random.split(key, 20)
    f32 = jnp.float32
    nrm = lambda k, shape, s: jax.random.normal(k, shape, f32) * s
    gain = lambda k, shape: 1.0 + 0.05 * jax.random.normal(k, shape, f32)
    return {
        "x": jax.random.normal(ks[0], (BATCH, SEQ, D_MODEL), f32),
        "norm_mix_pre": gain(ks[1], (DEPTH, D_MODEL)),
        "w_in": nrm(ks[2], (DEPTH, D_MODEL, IN_WIDTH), D_MODEL ** -0.5),
        "conv_w": nrm(ks[3], (DEPTH, CONV_WIDTH, CONV_CH), CONV_WIDTH ** -0.5),
        "conv_b": nrm(ks[4], (DEPTH, CONV_CH), 0.01),
        "b_igate": nrm(ks[5], (DEPTH, N_MLSTM_HEADS), 0.1),
        "b_fgate": 3.0 + nrm(ks[6], (DEPTH, N_MLSTM_HEADS), 0.5),
        "lambda_q1": nrm(ks[7], (DEPTH, DIFF_HEAD_DIM), 0.1),
        "lambda_k1": nrm(ks[8], (DEPTH, DIFF_HEAD_DIM), 0.1),
        "lambda_q2": nrm(ks[9], (DEPTH, DIFF_HEAD_DIM), 0.1),
        "lambda_k2": nrm(ks[10], (DEPTH, DIFF_HEAD_DIM), 0.1),
        "diff_norm": gain(ks[11], (DEPTH, DIFF_V_DIM)),
        "mlstm_norm": gain(ks[12], (DEPTH, N_MLSTM_HEADS, MLSTM_V_DIM)),
        "w_out": nrm(ks[13], (DEPTH, MIX_WIDTH, D_MODEL), MIX_WIDTH ** -0.5),
        "norm_mix_post": gain(ks[14], (DEPTH, D_MODEL)),
        "norm_mlp_pre": gain(ks[15], (DEPTH, D_MODEL)),
        "w_up": nrm(ks[16], (DEPTH, D_MODEL, D_FF), D_MODEL ** -0.5),
        "w_down": nrm(ks[17], (DEPTH, D_FF, D_MODEL), D_FF ** -0.5),
        "norm_mlp_post": gain(ks[18], (DEPTH, D_MODEL)),
    }


def reference(x, norm_mix_pre, w_in, conv_w, conv_b, b_igate, b_fgate,
              lambda_q1, lambda_k1, lambda_q2, lambda_k2, diff_norm, mlstm_norm,
              w_out, norm_mix_post, norm_mlp_pre, w_up, w_down, norm_mlp_post):
    B, S, _ = x.shape
    pos = jnp.arange(S, dtype=jnp.int32)
    for l in range(DEPTH):
        lam_init = lambda_init(l)
        h = rmsnorm(x, norm_mix_pre[l])
        proj = h @ w_in[l]
        dq, dk, dvv, mq, mk, mv, mo, mi, mf = jnp.split(proj, IN_OFFSETS, axis=-1)

        dq = rope(dq.reshape(B, S, 2 * N_DIFF_HEADS, DIFF_HEAD_DIM), pos)
        dk = rope(dk.reshape(B, S, 2 * N_DIFF_HEADS, DIFF_HEAD_DIM), pos)
        dq = dq.reshape(B, S, N_DIFF_HEADS, 2, DIFF_HEAD_DIM).transpose(0, 2, 3, 1, 4)
        dk = dk.reshape(B, S, N_DIFF_HEADS, 2, DIFF_HEAD_DIM).transpose(0, 2, 3, 1, 4)
        dvv = dvv.reshape(B, S, N_DIFF_HEADS, DIFF_V_DIM).transpose(0, 2, 1, 3)
        lam = (jnp.exp(jnp.sum(lambda_q1[l].astype(jnp.float32) * lambda_k1[l].astype(jnp.float32)))
               - jnp.exp(jnp.sum(lambda_q2[l].astype(jnp.float32) * lambda_k2[l].astype(jnp.float32)))
               + lam_init)
        o_diff = diff_attention(dq, dk, dvv, lam)
        o_diff = rmsnorm(o_diff, diff_norm[l]) * (1.0 - lam_init)
        o_diff = o_diff.transpose(0, 2, 1, 3).reshape(B, S, DIFF_WIDTH)

        qk = causal_dwconv(jnp.concatenate([mq, mk], axis=-1), conv_w[l], conv_b[l])
        qk = jax.nn.silu(qk)
        mq, mk = qk[..., :CONV_CH // 2], qk[..., CONV_CH // 2:]
        to_heads = lambda t, d: t.reshape(B, S, N_MLSTM_HEADS, d).transpose(0, 2, 1, 3).astype(jnp.float32)
        i_pre = (mi + b_igate[l]).astype(jnp.float32).transpose(0, 2, 1)
        logf = jax.nn.log_sigmoid((mf + b_fgate[l]).astype(jnp.float32)).transpose(0, 2, 1)
        hm = mlstm_chunkwise(to_heads(mq, MLSTM_QK_DIM), to_heads(mk, MLSTM_QK_DIM),
                             to_heads(mv, MLSTM_V_DIM), i_pre, logf)
        hm = rmsnorm(hm.transpose(0, 2, 1, 3), mlstm_norm[l]).astype(x.dtype)
        o_mlstm = hm.reshape(B, S, MLSTM_WIDTH) * jax.nn.sigmoid(mo)

        mixed = jnp.concatenate([o_diff.astype(x.dtype), o_mlstm], axis=-1) @ w_out[l]
        x = x + rmsnorm(mixed, norm_mix_post[l])

        h = rmsnorm(x, norm_mlp_pre[l])
        u = jnp.square(jax.nn.relu(h @ w_up[l]))
        x = x + rmsnorm(u @ w_down[l], norm_mlp_post[l])
    return x
```

```python
import functools
import math

import numpy as np
import jax
import jax.numpy as jnp
from jax import lax
from jax.experimental import pallas as pl
from jax.experimental.pallas import tpu as pltpu

F32 = jnp.float32
BF16 = jnp.bfloat16

N_DIFF_HEADS = 4
DIFF_HEAD_DIM = 64
N_MLSTM_HEADS = 4
MLSTM_QK_DIM = 64
MLSTM_V_DIM = 128
CONV_WIDTH = 4
ROPE_THETA = 10000.0
EPS = 1e-6

LANES = 128
SUBLANES = 8
VMEM_LIMIT_BYTES = 56 * 1024 * 1024

IN_PROJ_ROWS = 512
ATTN_Q_ROWS = 256
ATTN_KV_ROWS = 256
MLSTM_CHUNK = 128
OUT_PROJ_ROWS = 512
MLP_ROWS = 512
MLP_FF_CHUNK = 1024


def _lambda_init(layer):
    return 0.8 - 0.6 * math.exp(-0.3 * layer)


def _params(semantics):
    return pltpu.CompilerParams(dimension_semantics=semantics,
                                vmem_limit_bytes=VMEM_LIMIT_BYTES)


def _rms(x, g):
    return x * lax.rsqrt(jnp.mean(x * x, axis=-1, keepdims=True) + EPS) * g


def _in_proj_kernel(x_ref, g_ref, w_ref, cos_ref, slo_ref, shi_ref, cw_ref, cb_ref,
                    qs_ref, dq_ref, dk_ref, dv_ref, mqk_ref, mv_ref, mo_ref,
                    gate_ref, carry_ref, *, tiles_per_seq):
    tm = x_ref.shape[0]
    i = pl.program_id(0)
    h = _rms(x_ref[...], g_ref[...]).astype(BF16)

    def proj(lo, hi):
        return jnp.dot(h, w_ref[:, lo:hi], preferred_element_type=F32)

    cos = cos_ref[...]
    s_lo = slo_ref[...]
    s_hi = shi_ref[...]

    def rope(p):
        outs = []
        for gi in range(p.shape[1] // LANES):
            v = p[:, gi * LANES:(gi + 1) * LANES]
            outs.append(v * cos + pltpu.roll(v, LANES - 32, 1) * s_lo
                        + pltpu.roll(v, 32, 1) * s_hi)
        return jnp.concatenate(outs, axis=1)

    dq_ref[...] = (rope(proj(0, 512)) * (DIFF_HEAD_DIM ** -0.5)).astype(BF16)
    dk_ref[...] = rope(proj(512, 1024)).astype(BF16)
    dv_ref[...] = proj(1024, 1536).astype(BF16)

    pre = proj(1536, 2048)

    @pl.when(i % tiles_per_seq == 0)
    def _():
        carry_ref[0:SUBLANES, :] = jnp.zeros((SUBLANES, pre.shape[1]), F32)

    carry_ref[SUBLANES:SUBLANES + tm, :] = pre
    cw = cw_ref[...]
    y = cb_ref[...] + cw[3:4, :] * pre
    for j in range(1, CONV_WIDTH):
        y = y + cw[3 - j:4 - j, :] * carry_ref[SUBLANES - j:SUBLANES - j + tm, :]
    carry_ref[0:SUBLANES, :] = pre[tm - SUBLANES:tm, :]
    mqk_ref[...] = y * jax.nn.sigmoid(y) * qs_ref[...]

    mv_ref[...] = proj(2048, 2560).astype(BF16)
    mo_ref[...] = proj(2560, 3072).astype(BF16)
    gate_ref[0] = proj(3072, 3200).T[0:SUBLANES, :]


def _in_proj(x2, g, w_main, cos_t, slo_t, shi_t, cw, cb, qs, *, batch, seq):
    t, d = x2.shape
    tm = IN_PROJ_ROWS
    tps = seq // tm
    nw = w_main.shape[1]
    row = lambda i: (i, 0)
    const = lambda i: (0, 0)
    pos = lambda i: (i % tps, 0)
    out_bf = jax.ShapeDtypeStruct((t, 512), BF16)
    return pl.pallas_call(
        functools.partial(_in_proj_kernel, tiles_per_seq=tps),
        grid=(t // tm,),
        in_specs=[
            pl.BlockSpec((tm, d), row),
            pl.BlockSpec((1, d), const),
            pl.BlockSpec((d, nw), const),
            pl.BlockSpec((tm, LANES), pos),
            pl.BlockSpec((tm, LANES), pos),
            pl.BlockSpec((tm, LANES), pos),
            pl.BlockSpec((CONV_WIDTH, 512), const),
            pl.BlockSpec((1, 512), const),
            pl.BlockSpec((1, 512), const),
        ],
        out_specs=[
            pl.BlockSpec((tm, 512), row),
            pl.BlockSpec((tm, 512), row),
            pl.BlockSpec((tm, 512), row),
            pl.BlockSpec((tm, 512), row),
            pl.BlockSpec((tm, 512), row),
            pl.BlockSpec((tm, 512), row),
            pl.BlockSpec((1, SUBLANES, tm), lambda i: (i // tps, 0, i % tps)),
        ],
        out_shape=[out_bf, out_bf, out_bf,
                   jax.ShapeDtypeStruct((t, 512), F32), out_bf, out_bf,
                   jax.ShapeDtypeStruct((batch, SUBLANES, seq), F32)],
        scratch_shapes=[pltpu.VMEM((tm + SUBLANES, 512), F32)],
        compiler_params=_params(("arbitrary",)),
        name="in_proj",
    )(x2, g, w_main, cos_t, slo_t, shi_t, cw, cb, qs)


def _diff_attn_kernel(lam_ref, gn_ref, q_ref, k_ref, v_ref, o_ref, *, lam_init):
    tq = q_ref.shape[1]
    tk = ATTN_KV_ROWS
    i = pl.program_id(2)
    lp = lam_ref[...]
    lam = (jnp.exp(jnp.sum(lp[0:1] * lp[1:2], axis=-1, keepdims=True))
           - jnp.exp(jnp.sum(lp[2:3] * lp[3:4], axis=-1, keepdims=True)) + lam_init)

    q = q_ref[0]
    lane = lax.broadcasted_iota(jnp.int32, (1, LANES), 1)
    zero = jnp.zeros_like(q)
    qa = jnp.where(lane < DIFF_HEAD_DIM, q, zero)
    qb = jnp.where(lane >= DIFF_HEAD_DIM, q, zero)
    nt = (((1,), (1,)), ((), ()))

    def update(s, vb, m, l, acc):
        m_new = jnp.maximum(m, jnp.max(s, axis=-1, keepdims=True))
        alpha = jnp.exp(m - m_new)
        p = jnp.exp(s - m_new)
        l = alpha * l + jnp.sum(p, axis=-1, keepdims=True)
        acc = alpha * acc + jnp.dot(p.astype(BF16), vb, preferred_element_type=F32)
        return m_new, l, acc

    def step(j, carry, masked):
        ma, la, acca, mb, lb, accb = carry
        start = pl.multiple_of(j * tk, tk)
        kb = k_ref[0, pl.ds(start, tk), :]
        vb = v_ref[0, pl.ds(start, tk), :]
        sa = lax.dot_general(qa, kb, nt, preferred_element_type=F32)
        sb = lax.dot_general(qb, kb, nt, preferred_element_type=F32)
        if masked:
            r = lax.broadcasted_iota(jnp.int32, (tq, tk), 0)
            c = lax.broadcasted_iota(jnp.int32, (tq, tk), 1)
            keep = r >= c
            sa = jnp.where(keep, sa, -jnp.inf)
            sb = jnp.where(keep, sb, -jnp.inf)
        ma, la, acca = update(sa, vb, ma, la, acca)
        mb, lb, accb = update(sb, vb, mb, lb, accb)
        return ma, la, acca, mb, lb, accb

    neg = jnp.full((tq, 1), -jnp.inf, F32)
    z1 = jnp.zeros((tq, 1), F32)
    za = jnp.zeros((tq, LANES), F32)
    carry = (neg, z1, za, neg, z1, za)
    carry = lax.fori_loop(0, i, lambda j, c: step(j, c, False), carry)
    ma, la, acca, mb, lb, accb = step(i, carry, True)
    o = acca / la - lam * (accb / lb)
    o_ref[0] = (_rms(o, gn_ref[...]) * (1.0 - lam_init)).astype(o_ref.dtype)


def _diff_attn(lam_p, gn, dq, dk, dv, *, lam_init):
    b, s, w = dq.shape
    nh = w // LANES
    tq = ATTN_Q_ROWS
    assert tq == ATTN_KV_ROWS
    return pl.pallas_call(
        functools.partial(_diff_attn_kernel, lam_init=lam_init),
        grid=(b, nh, s // tq),
        in_specs=[
            pl.BlockSpec((4, DIFF_HEAD_DIM), lambda bi, hi, qi: (0, 0)),
            pl.BlockSpec((1, LANES), lambda bi, hi, qi: (0, 0)),
            pl.BlockSpec((1, tq, LANES), lambda bi, hi, qi: (bi, qi, hi)),
            pl.BlockSpec((1, s, LANES), lambda bi, hi, qi: (bi, 0, hi)),
            pl.BlockSpec((1, s, LANES), lambda bi, hi, qi: (bi, 0, hi)),
        ],
        out_specs=pl.BlockSpec((1, tq, LANES), lambda bi, hi, qi: (bi, qi, hi)),
        out_shape=jax.ShapeDtypeStruct((b, s, w), BF16),
        compiler_params=_params(("arbitrary", "arbitrary", "arbitrary")),
        name="diff_attn",
    )(lam_p, gn, dq, dk, dv)


def _seg_scan(x, seg_off, seg_len, op, fill, reverse=False):
    n = x.shape[1]
    d = 1
    while d < seg_len:
        if reverse:
            sh = pltpu.roll(x, n - d, 1)
            ok = seg_off < seg_len - d
        else:
            sh = pltpu.roll(x, d, 1)
            ok = seg_off >= d
        x = op(x, jnp.where(ok, sh, fill))
        d *= 2
    return x


def _log_sigmoid(x):
    return -(jnp.maximum(-x, 0.0) + jnp.log1p(jnp.exp(-jnp.abs(x))))


def _mlstm_kernel(gate_ref, bias_ref, qk_ref, v_ref, o_in_ref, gn_ref, out_ref,
                  arow_ref, rowbuf_ref, colbuf_ref, c_ref):
    s = qk_ref.shape[1]
    lc = MLSTM_CHUNK
    nc = s // lc
    nh = N_MLSTM_HEADS

    g = gate_ref[0] + bias_ref[...]
    i_pre = g
    logf = _log_sigmoid(pltpu.roll(g, nh, 0))
    pos = lax.broadcasted_iota(jnp.int32, (1, s), 1)
    seg = pos % lc
    last = seg == lc - 1
    add = lambda a, b_: a + b_
    bcum = _seg_scan(logf, seg, lc, add, 0.0)
    a = i_pre - bcum
    cmax = _seg_scan(a, seg, lc, jnp.maximum, -jnp.inf)
    gb = _seg_scan(jnp.where(last, bcum, 0.0), seg, lc, add, 0.0, reverse=True)
    xb = gb + _seg_scan(jnp.where(last, cmax, -jnp.inf), seg, lc, jnp.maximum,
                        -jnp.inf, reverse=True)
    m_prev = jnp.zeros((SUBLANES, lc), F32)
    mp, mc = [], []
    for c in range(nc):
        m_cur = jnp.maximum(gb[:, c * lc:(c + 1) * lc] + m_prev, xb[:, c * lc:(c + 1) * lc])
        mp.append(m_prev)
        mc.append(m_cur)
        m_prev = m_cur
    mprev = jnp.concatenate(mp, axis=1)
    mcur = jnp.concatenate(mc, axis=1)
    u = jnp.maximum(mprev, cmax)
    rowbuf_ref[0:8, :] = u
    rowbuf_ref[8:16, :] = jnp.exp(mprev - u)
    rowbuf_ref[16:24, :] = jnp.exp(-(u + bcum))
    rowbuf_ref[24:32, :] = jnp.exp(gb + a - mcur)
    rowbuf_ref[32:40, :] = jnp.exp(gb + mprev - mcur)
    rowbuf_ref[40:LANES, :] = jnp.zeros((LANES - 40, s), F32)
    for c in range(nc):
        arow_ref[c] = a[:, c * lc:(c + 1) * lc]
        colbuf_ref[c * lc:(c + 1) * lc, :] = rowbuf_ref[:, c * lc:(c + 1) * lc].T

    c_ref[...] = jnp.zeros(c_ref.shape, F32)
    lane = lax.broadcasted_iota(jnp.int32, (1, LANES), 1)
    ones_col = jnp.where(lane == 0, 1.0, 0.0).astype(BF16)
    r = lax.broadcasted_iota(jnp.int32, (lc, lc), 0)
    cc = lax.broadcasted_iota(jnp.int32, (lc, lc), 1)
    tril = r >= cc
    nt = (((1,), (1,)), ((), ()))
    tn = (((0,), (0,)), ((), ()))
    gn = gn_ref[...]

    def chunk(c, _):
        start = pl.multiple_of(c * lc, lc)
        cols = colbuf_ref[pl.ds(start, lc), :]
        arow = arow_ref[c]
        for hd in range(nh):
            sl = slice(hd * LANES, (hd + 1) * LANES)
            qk = qk_ref[0, pl.ds(start, lc), sl]
            kr = pltpu.roll(qk, MLSTM_QK_DIM, 1)
            qm = jnp.where(lane < MLSTM_QK_DIM, qk, 0.0).astype(BF16)
            vaug = jnp.concatenate(
                [v_ref[0, pl.ds(start, lc), sl], jnp.broadcast_to(ones_col, (lc, LANES))],
                axis=1)
            sqk = lax.dot_general(qm, kr.astype(BF16), nt, preferred_element_type=F32)
            arg = jnp.where(tril, arow[hd:hd + 1, :] - cols[:, hd:hd + 1], -jnp.inf)
            p = (sqk * jnp.exp(arg)).astype(BF16)
            cst = c_ref[hd]
            nd = (cols[:, 8 + hd:9 + hd]
                  * jnp.dot(qm, cst.astype(BF16), preferred_element_type=F32)
                  + jnp.dot(p, vaug, preferred_element_type=F32))
            den = jnp.maximum(jnp.abs(nd[:, LANES:LANES + 1]), cols[:, 16 + hd:17 + hd])
            hv = nd[:, 0:LANES] / den
            hn = _rms(hv, gn[:, sl])
            og = o_in_ref[0, pl.ds(start, lc), sl].astype(F32)
            out_ref[0, pl.ds(start, lc), sl] = (hn * jax.nn.sigmoid(og)).astype(out_ref.dtype)
            kw = (kr * cols[:, 24 + hd:25 + hd]).astype(BF16)
            c_ref[hd] = (cols[0:1, 32 + hd:33 + hd] * cst
                         + lax.dot_general(kw, vaug, tn, preferred_element_type=F32))
        return 0

    lax.fori_loop(0, nc, chunk, 0)


def _mlstm(gates, bias, mqk, mv, mo, gn):
    b, s, w = mqk.shape
    lc = MLSTM_CHUNK
    seq = lambda bi: (bi, 0, 0)
    return pl.pallas_call(
        _mlstm_kernel,
        grid=(b,),
        in_specs=[
            pl.BlockSpec((1, SUBLANES, s), seq),
            pl.BlockSpec((SUBLANES, 1), lambda bi: (0, 0)),
            pl.BlockSpec((1, s, w), seq),
            pl.BlockSpec((1, s, w), seq),
            pl.BlockSpec((1, s, w), seq),
            pl.BlockSpec((1, w), lambda bi: (0, 0)),
        ],
        out_specs=pl.BlockSpec((1, s, w), seq),
        out_shape=jax.ShapeDtypeStruct((b, s, w), BF16),
        scratch_shapes=[
            pltpu.VMEM((s // lc, SUBLANES, lc), F32),
            pltpu.VMEM((LANES, s), F32),
            pltpu.VMEM((s, LANES), F32),
            pltpu.VMEM((N_MLSTM_HEADS, LANES, 2 * LANES), F32),
        ],
        compiler_params=_params(("arbitrary",)),
        name="mlstm",
    )(gates, bias, mqk, mv, mo, gn)


def _out_proj_kernel(x_ref, od_ref, om_ref, wd_ref, wm_ref, gpost_ref, gpre_ref,
                     x1_ref, h2_ref):
    mixed = (jnp.dot(od_ref[...], wd_ref[...], preferred_element_type=F32)
             + jnp.dot(om_ref[...], wm_ref[...], preferred_element_type=F32))
    x1 = x_ref[...] + _rms(mixed, gpost_ref[...])
    x1_ref[...] = x1
    h2_ref[...] = _rms(x1, gpre_ref[...]).astype(BF16)


def _out_proj(x2, od, om, wd, wm, gpost, gpre):
    t, d = x2.shape
    tm = OUT_PROJ_ROWS
    row = lambda i: (i, 0)
    const = lambda i: (0, 0)
    return pl.pallas_call(
        _out_proj_kernel,
        grid=(t // tm,),
        in_specs=[
            pl.BlockSpec((tm, d), row),
            pl.BlockSpec((tm, od.shape[1]), row),
            pl.BlockSpec((tm, om.shape[1]), row),
            pl.BlockSpec(wd.shape, const),
            pl.BlockSpec(wm.shape, const),
            pl.BlockSpec((1, d), const),
            pl.BlockSpec((1, d), const),
        ],
        out_specs=[pl.BlockSpec((tm, d), row), pl.BlockSpec((tm, d), row)],
        out_shape=[jax.ShapeDtypeStruct((t, d), F32), jax.ShapeDtypeStruct((t, d), BF16)],
        compiler_params=_params(("arbitrary",)),
        name="out_proj",
    )(x2, od, om, wd, wm, gpost, gpre)


def _mlp_kernel(x1_ref, h2_ref, wu_ref, wd_ref, g_ref, o_ref):
    h = h2_ref[...]
    dff = wu_ref.shape[1]
    acc = None
    for f in range(dff // MLP_FF_CHUNK):
        sl = slice(f * MLP_FF_CHUNK, (f + 1) * MLP_FF_CHUNK)
        a = jnp.maximum(jnp.dot(h, wu_ref[:, sl], preferred_element_type=F32), 0.0)
        part = jnp.dot((a * a).astype(BF16), wd_ref[sl, :], preferred_element_type=F32)
        acc = part if acc is None else acc + part
    o_ref[...] = x1_ref[...] + _rms(acc, g_ref[...])


def _mlp(x1, h2, wu, wd, g):
    t, d = x1.shape
    tm = MLP_ROWS
    row = lambda i: (i, 0)
    const = lambda i: (0, 0)
    return pl.pallas_call(
        _mlp_kernel,
        grid=(t // tm,),
        in_specs=[
            pl.BlockSpec((tm, d), row),
            pl.BlockSpec((tm, d), row),
            pl.BlockSpec(wu.shape, const),
            pl.BlockSpec(wd.shape, const),
            pl.BlockSpec((1, d), const),
        ],
        out_specs=pl.BlockSpec((tm, d), row),
        out_shape=jax.ShapeDtypeStruct((t, d), F32),
        compiler_params=_params(("arbitrary",)),
        name="mlp",
    )(x1, h2, wu, wd, g)


def _rope_tables(seq):
    d = DIFF_HEAD_DIM
    inv = ROPE_THETA ** (-jnp.arange(0, d, 2, dtype=F32) / d)
    ang = jnp.arange(seq, dtype=jnp.int32).astype(F32)[:, None] * inv[None, :]
    cos = jnp.cos(ang)
    sin = jnp.sin(ang)
    zero = jnp.zeros_like(sin)
    reps = LANES // d
    cos_t = jnp.tile(jnp.concatenate([cos, cos], axis=1), (1, reps))
    slo_t = jnp.tile(jnp.concatenate([-sin, zero], axis=1), (1, reps))
    shi_t = jnp.tile(jnp.concatenate([zero, sin], axis=1), (1, reps))
    return cos_t, slo_t, shi_t


def _layer(x, l, norm_mix_pre, w_in, conv_w, conv_b, b_igate, b_fgate, lambda_q1,
           lambda_k1, lambda_q2, lambda_k2, diff_norm, mlstm_norm, w_out,
           norm_mix_post, norm_mlp_pre, w_up, w_down, norm_mlp_post):
    b, s, d = x.shape
    nh, dqk = N_MLSTM_HEADS, MLSTM_QK_DIM
    x2 = x.reshape(b * s, d)

    w = w_in[l]
    wq = w[:, 1536:1792].reshape(d, nh, dqk)
    wk = w[:, 1792:2048].reshape(d, nh, dqk)
    wqk = jnp.concatenate([wq, wk], axis=2).reshape(d, 2 * nh * dqk)
    wg = jnp.pad(w[:, 3072:3080], ((0, 0), (0, LANES - 2 * nh)))
    w_main = jnp.concatenate([w[:, 0:1536], wqk, w[:, 2048:3072], wg], axis=1).astype(BF16)

    def qk_interleave(v):
        lead = v.shape[:-1]
        q = v[..., :nh * dqk].reshape(lead + (nh, dqk))
        k = v[..., nh * dqk:].reshape(lead + (nh, dqk))
        return jnp.concatenate([q, k], axis=-1).reshape(lead + (2 * nh * dqk,))

    cw = qk_interleave(conv_w[l])
    cb = qk_interleave(conv_b[l])[None, :]
    qs = qk_interleave(jnp.concatenate([jnp.full((nh * dqk,), dqk ** -0.5, F32),
                                        jnp.ones((nh * dqk,), F32)]))[None, :]
    cos_t, slo_t, shi_t = _rope_tables(s)

    dq, dk, dv, mqk, mv, mo, gates = _in_proj(
        x2, norm_mix_pre[l][None, :], w_main, cos_t, slo_t, shi_t, cw, cb, qs,
        batch=b, seq=s)

    lam_p = jnp.stack([lambda_q1[l], lambda_k1[l], lambda_q2[l], lambda_k2[l]]).astype(F32)
    o_diff = _diff_attn(lam_p, diff_norm[l][None, :], dq.reshape(b, s, -1),
                        dk.reshape(b, s, -1), dv.reshape(b, s, -1),
                        lam_init=_lambda_init(l))

    bias = jnp.concatenate([b_igate[l], b_fgate[l]]).astype(F32)[:, None]
    o_mlstm = _mlstm(gates, bias, mqk.reshape(b, s, -1), mv.reshape(b, s, -1),
                     mo.reshape(b, s, -1), mlstm_norm[l].reshape(1, -1))

    wo = w_out[l].astype(BF16)
    nd = o_diff.shape[-1]
    x1, h2 = _out_proj(x2, o_diff.reshape(b * s, -1), o_mlstm.reshape(b * s, -1),
                       wo[:nd], wo[nd:], norm_mix_post[l][None, :],
                       norm_mlp_pre[l][None, :])
    out = _mlp(x1, h2, w_up[l].astype(BF16), w_down[l].astype(BF16),
               norm_mlp_post[l][None, :])
    return out.reshape(b, s, d)


def kernel(x, norm_mix_pre, w_in, conv_w, conv_b, b_igate, b_fgate, lambda_q1, lambda_k1,
           lambda_q2, lambda_k2, diff_norm, mlstm_norm, w_out, norm_mix_post,
           norm_mlp_pre, w_up, w_down, norm_mlp_post):
    for l in range(w_in.shape[0]):
        x = _layer(x, l, norm_mix_pre, w_in, conv_w, conv_b, b_igate, b_fgate,
                   lambda_q1, lambda_k1, lambda_q2, lambda_k2, diff_norm, mlstm_norm,
                   w_out, norm_mix_post, norm_mlp_pre, w_up, w_down, norm_mlp_post)
    return x
```

```python
import functools
import math

import numpy as np
import jax
import jax.numpy as jnp
from jax import lax
from jax.experimental import pallas as pl
from jax.experimental.pallas import tpu as pltpu

F32 = jnp.float32
BF16 = jnp.bfloat16

N_DIFF_HEADS = 4
DIFF_HEAD_DIM = 64
N_MLSTM_HEADS = 4
MLSTM_QK_DIM = 64
MLSTM_V_DIM = 128
CONV_WIDTH = 4
ROPE_THETA = 10000.0
EPS = 1e-6

LANES = 128
SUBLANES = 8
VMEM_LIMIT_BYTES = 56 * 1024 * 1024

IN_PROJ_ROWS = 512
ATTN_Q_ROWS = 256
ATTN_KV_ROWS = 256
MLSTM_CHUNK = 128
OUT_PROJ_ROWS = 512
MLP_ROWS = 512
MLP_FF_CHUNK = 1024


def _lambda_init(layer):
    return 0.8 - 0.6 * math.exp(-0.3 * layer)


def _params(semantics):
    return pltpu.CompilerParams(dimension_semantics=semantics,
                                vmem_limit_bytes=VMEM_LIMIT_BYTES)


def _rms(x, g):
    return x * lax.rsqrt(jnp.mean(x * x, axis=-1, keepdims=True) + EPS) * g


def _in_proj_kernel(x_ref, g_ref, w_ref, cos_ref, slo_ref, shi_ref, cw_ref, cb_ref,
                    qs_ref, dq_ref, dk_ref, dv_ref, mqk_ref, mv_ref, mo_ref,
                    gate_ref, carry_ref, *, tiles_per_seq):
    tm = x_ref.shape[0]
    i = pl.program_id(0)
    h = _rms(x_ref[...], g_ref[...]).astype(BF16)

    def proj(lo, hi):
        return jnp.dot(h, w_ref[:, lo:hi], preferred_element_type=F32)

    cos = cos_ref[...]
    s_lo = slo_ref[...]
    s_hi = shi_ref[...]

    def rope(p):
        outs = []
        for gi in range(p.shape[1] // LANES):
            v = p[:, gi * LANES:(gi + 1) * LANES]
            outs.append(v * cos + pltpu.roll(v, LANES - 32, 1) * s_lo
                        + pltpu.roll(v, 32, 1) * s_hi)
        return jnp.concatenate(outs, axis=1)

    dq_ref[...] = (rope(proj(0, 512)) * (DIFF_HEAD_DIM ** -0.5)).astype(BF16)
    dk_ref[...] = rope(proj(512, 1024)).astype(BF16)
    dv_ref[...] = proj(1024, 1536).astype(BF16)

    pre = proj(1536, 2048)

    @pl.when(i % tiles_per_seq == 0)
    def _():
        carry_ref[0:SUBLANES, :] = jnp.zeros((SUBLANES, pre.shape[1]), F32)

    carry_ref[SUBLANES:SUBLANES + tm, :] = pre
    cw = cw_ref[...]
    y = cb_ref[...] + cw[3:4, :] * pre
    for j in range(1, CONV_WIDTH):
        y = y + cw[3 - j:4 - j, :] * carry_ref[SUBLANES - j:SUBLANES - j + tm, :]
    carry_ref[0:SUBLANES, :] = pre[tm - SUBLANES:tm, :]
    mqk_ref[...] = y * jax.nn.sigmoid(y) * qs_ref[...]

    mv_ref[...] = proj(2048, 2560).astype(BF16)
    mo_ref[...] = proj(2560, 3072).astype(BF16)
    gate_ref[0] = proj(3072, 3200).T[0:SUBLANES, :]


def _in_proj(x2, g, w_main, cos_t, slo_t, shi_t, cw, cb, qs, *, batch, seq):
    t, d = x2.shape
    tm = IN_PROJ_ROWS
    tps = seq // tm
    nw = w_main.shape[1]
    row = lambda i: (i, 0)
    const = lambda i: (0, 0)
    pos = lambda i: (i % tps, 0)
    out_bf = jax.ShapeDtypeStruct((t, 512), BF16)
    return pl.pallas_call(
        functools.partial(_in_proj_kernel, tiles_per_seq=tps),
        grid=(t // tm,),
        in_specs=[
            pl.BlockSpec((tm, d), row),
            pl.BlockSpec((1, d), const),
            pl.BlockSpec((d, nw), const),
            pl.BlockSpec((tm, LANES), pos),
            pl.BlockSpec((tm, LANES), pos),
            pl.BlockSpec((tm, LANES), pos),
            pl.BlockSpec((CONV_WIDTH, 512), const),
            pl.BlockSpec((1, 512), const),
            pl.BlockSpec((1, 512), const),
        ],
        out_specs=[
            pl.BlockSpec((tm, 512), row),
            pl.BlockSpec((tm, 512), row),
            pl.BlockSpec((tm, 512), row),
            pl.BlockSpec((tm, 512), row),
            pl.BlockSpec((tm, 512), row),
            pl.BlockSpec((tm, 512), row),
            pl.BlockSpec((1, SUBLANES, tm), lambda i: (i // tps, 0, i % tps)),
        ],
        out_shape=[out_bf, out_bf, out_bf,
                   jax.ShapeDtypeStruct((t, 512), F32), out_bf, out_bf,
                   jax.ShapeDtypeStruct((batch, SUBLANES, seq), F32)],
        scratch_shapes=[pltpu.VMEM((tm + SUBLANES, 512), F32)],
        compiler_params=_params(("arbitrary",)),
        name="in_proj",
    )(x2, g, w_main, cos_t, slo_t, shi_t, cw, cb, qs)


def _diff_attn_kernel(lam_ref, gn_ref, q_ref, k_ref, v_ref, o_ref,
                      acc_ref, m_ref, l_ref, *, lam_init):
    s = q_ref.shape[1]
    tk = ATTN_KV_ROWS
    tf = ATTN_Q_ROWS
    lp = lam_ref[...]
    lam = (jnp.exp(jnp.sum(lp[0:1] * lp[1:2], axis=-1, keepdims=True))
           - jnp.exp(jnp.sum(lp[2:3] * lp[3:4], axis=-1, keepdims=True)) + lam_init)

    lane = lax.broadcasted_iota(jnp.int32, (1, LANES), 1)
    nt = (((1,), (1,)), ((), ()))
    keep = (lax.broadcasted_iota(jnp.int32, (tk, tk), 1)
            >= lax.broadcasted_iota(jnp.int32, (tk, tk), 0))
    q = q_ref[0]
    zero = jnp.zeros_like(q)
    qmaps = (jnp.where(lane < DIFF_HEAD_DIM, q, zero),
             jnp.where(lane >= DIFF_HEAD_DIM, q, zero))

    m_ref[...] = jnp.full(m_ref.shape, -jnp.inf, F32)
    l_ref[...] = jnp.zeros(l_ref.shape, F32)
    acc_ref[...] = jnp.zeros(acc_ref.shape, F32)

    for j in range(s // tk):
        lo = j * tk
        kb = k_ref[0, lo:lo + tk, :]
        vt = jnp.concatenate(
            [v_ref[0, lo + c * LANES:lo + (c + 1) * LANES, :].astype(F32).T
             for c in range(tk // LANES)], axis=1).astype(BF16)
        for mi, qm in enumerate(qmaps):
            cols = slice(mi * s + lo, mi * s + s)
            st = lax.dot_general(kb, qm[lo:, :], nt, preferred_element_type=F32)
            diag = jnp.where(keep, st[:, :tk], -jnp.inf)
            st = diag if s - lo == tk else jnp.concatenate([diag, st[:, tk:]], axis=1)
            m_old = m_ref[:, cols]
            m_new = jnp.maximum(m_old, jnp.max(st, axis=0, keepdims=True))
            alpha = jnp.exp(m_old - m_new)
            p = jnp.exp(st - m_new)
            l_ref[:, cols] = alpha * l_ref[:, cols] + jnp.sum(p, axis=0, keepdims=True)
            m_ref[:, cols] = m_new
            acc_ref[:, cols] = alpha * acc_ref[:, cols] + jnp.dot(
                vt, p.astype(BF16), preferred_element_type=F32)

    for c0 in range(0, s, tf):
        a1 = acc_ref[:, c0:c0 + tf] / l_ref[:, c0:c0 + tf]
        a2 = acc_ref[:, s + c0:s + c0 + tf] / l_ref[:, s + c0:s + c0 + tf]
        ot = a1 - lam * a2
        ms = jnp.mean(ot * ot, axis=0, keepdims=True)
        on = ot * lax.rsqrt(ms + EPS) * gn_ref[...] * (1.0 - lam_init)
        o_ref[0, c0:c0 + tf, :] = on.T.astype(o_ref.dtype)


def _diff_attn(lam_p, gn_col, dq, dk, dv, *, lam_init):
    b, s, w = dq.shape
    nh = w // LANES
    head = lambda bi, hi: (bi, 0, hi)
    const = lambda bi, hi: (0, 0)
    return pl.pallas_call(
        functools.partial(_diff_attn_kernel, lam_init=lam_init),
        grid=(b, nh),
        in_specs=[
            pl.BlockSpec((4, DIFF_HEAD_DIM), const),
            pl.BlockSpec((LANES, 1), const),
            pl.BlockSpec((1, s, LANES), head),
            pl.BlockSpec((1, s, LANES), head),
            pl.BlockSpec((1, s, LANES), head),
        ],
        out_specs=pl.BlockSpec((1, s, LANES), head),
        out_shape=jax.ShapeDtypeStruct((b, s, w), BF16),
        scratch_shapes=[
            pltpu.VMEM((LANES, 2 * s), F32),
            pltpu.VMEM((1, 2 * s), F32),
            pltpu.VMEM((1, 2 * s), F32),
        ],
        compiler_params=_params(("arbitrary", "arbitrary")),
        name="diff_attn",
    )(lam_p, gn_col, dq, dk, dv)


def _seg_scan(x, seg_off, seg_len, op, fill, reverse=False):
    n = x.shape[1]
    d = 1
    while d < seg_len:
        if reverse:
            sh = pltpu.roll(x, n - d, 1)
            ok = seg_off < seg_len - d
        else:
            sh = pltpu.roll(x, d, 1)
            ok = seg_off >= d
        x = op(x, jnp.where(ok, sh, fill))
        d *= 2
    return x


def _log_sigmoid(x):
    return -(jnp.maximum(-x, 0.0) + jnp.log1p(jnp.exp(-jnp.abs(x))))


def _mlstm_kernel(gate_ref, bias_ref, qk_ref, v_ref, o_in_ref, gn_ref, out_ref,
                  arow_ref, rowbuf_ref, colbuf_ref, c_ref):
    s = qk_ref.shape[1]
    lc = MLSTM_CHUNK
    nc = s // lc
    nh = N_MLSTM_HEADS

    g = gate_ref[0] + bias_ref[...]
    i_pre = g
    logf = _log_sigmoid(pltpu.roll(g, nh, 0))
    pos = lax.broadcasted_iota(jnp.int32, (1, s), 1)
    seg = pos % lc
    last = seg == lc - 1
    add = lambda a, b_: a + b_
    bcum = _seg_scan(logf, seg, lc, add, 0.0)
    a = i_pre - bcum
    cmax = _seg_scan(a, seg, lc, jnp.maximum, -jnp.inf)
    gb = _seg_scan(jnp.where(last, bcum, 0.0), seg, lc, add, 0.0, reverse=True)
    xb = gb + _seg_scan(jnp.where(last, cmax, -jnp.inf), seg, lc, jnp.maximum,
                        -jnp.inf, reverse=True)
    m_prev = jnp.zeros((SUBLANES, lc), F32)
    mp, mc = [], []
    for c in range(nc):
        m_cur = jnp.maximum(gb[:, c * lc:(c + 1) * lc] + m_prev, xb[:, c * lc:(c + 1) * lc])
        mp.append(m_prev)
        mc.append(m_cur)
        m_prev = m_cur
    mprev = jnp.concatenate(mp, axis=1)
    mcur = jnp.concatenate(mc, axis=1)
    u = jnp.maximum(mprev, cmax)
    rowbuf_ref[0:8, :] = u
    rowbuf_ref[8:16, :] = jnp.exp(mprev - u)
    rowbuf_ref[16:24, :] = jnp.exp(-(u + bcum))
    rowbuf_ref[24:32, :] = jnp.exp(gb + a - mcur)
    rowbuf_ref[32:40, :] = jnp.exp(gb + mprev - mcur)
    rowbuf_ref[40:LANES, :] = jnp.zeros((LANES - 40, s), F32)
    for c in range(nc):
        arow_ref[c] = a[:, c * lc:(c + 1) * lc]
        colbuf_ref[c * lc:(c + 1) * lc, :] = rowbuf_ref[:, c * lc:(c + 1) * lc].T

    c_ref[...] = jnp.zeros(c_ref.shape, F32)
    lane = lax.broadcasted_iota(jnp.int32, (1, LANES), 1)
    ones_col = jnp.where(lane == 0, 1.0, 0.0).astype(BF16)
    r = lax.broadcasted_iota(jnp.int32, (lc, lc), 0)
    cc = lax.broadcasted_iota(jnp.int32, (lc, lc), 1)
    tril = r >= cc
    nt = (((1,), (1,)), ((), ()))
    tn = (((0,), (0,)), ((), ()))
    gn = gn_ref[...]

    def chunk(c, _):
        start = pl.multiple_of(c * lc, lc)
        cols = colbuf_ref[pl.ds(start, lc), :]
        arow = arow_ref[c]
        for hd in range(nh):
            sl = slice(hd * LANES, (hd + 1) * LANES)
            qk = qk_ref[0, pl.ds(start, lc), sl]
            kr = pltpu.roll(qk, MLSTM_QK_DIM, 1)
            qm = jnp.where(lane < MLSTM_QK_DIM, qk, 0.0).astype(BF16)
            vaug = jnp.concatenate(
                [v_ref[0, pl.ds(start, lc), sl], jnp.broadcast_to(ones_col, (lc, LANES))],
                axis=1)
            sqk = lax.dot_general(qm, kr.astype(BF16), nt, preferred_element_type=F32)
            arg = jnp.where(tril, arow[hd:hd + 1, :] - cols[:, hd:hd + 1], -jnp.inf)
            p = (sqk * jnp.exp(arg)).astype(BF16)
            cst = c_ref[hd]
            nd = (cols[:, 8 + hd:9 + hd]
                  * jnp.dot(qm, cst.astype(BF16), preferred_element_type=F32)
                  + jnp.dot(p, vaug, preferred_element_type=F32))
            den = jnp.maximum(jnp.abs(nd[:, LANES:LANES + 1]), cols[:, 16 + hd:17 + hd])
            hv = nd[:, 0:LANES] / den
            hn = _rms(hv, gn[:, sl])
            og = o_in_ref[0, pl.ds(start, lc), sl].astype(F32)
            out_ref[0, pl.ds(start, lc), sl] = (hn * jax.nn.sigmoid(og)).astype(out_ref.dtype)
            kw = (kr * cols[:, 24 + hd:25 + hd]).astype(BF16)
            c_ref[hd] = (cols[0:1, 32 + hd:33 + hd] * cst
                         + lax.dot_general(kw, vaug, tn, preferred_element_type=F32))
        return 0

    lax.fori_loop(0, nc, chunk, 0)


def _mlstm(gates, bias, mqk, mv, mo, gn):
    b, s, w = mqk.shape
    lc = MLSTM_CHUNK
    seq = lambda bi: (bi, 0, 0)
    return pl.pallas_call(
        _mlstm_kernel,
        grid=(b,),
        in_specs=[
            pl.BlockSpec((1, SUBLANES, s), seq),
            pl.BlockSpec((SUBLANES, 1), lambda bi: (0, 0)),
            pl.BlockSpec((1, s, w), seq),
            pl.BlockSpec((1, s, w), seq),
            pl.BlockSpec((1, s, w), seq),
            pl.BlockSpec((1, w), lambda bi: (0, 0)),
        ],
        out_specs=pl.BlockSpec((1, s, w), seq),
        out_shape=jax.ShapeDtypeStruct((b, s, w), BF16),
        scratch_shapes=[
            pltpu.VMEM((s // lc, SUBLANES, lc), F32),
            pltpu.VMEM((LANES, s), F32),
            pltpu.VMEM((s, LANES), F32),
            pltpu.VMEM((N_MLSTM_HEADS, LANES, 2 * LANES), F32),
        ],
        compiler_params=_params(("arbitrary",)),
        name="mlstm",
    )(gates, bias, mqk, mv, mo, gn)


def _out_proj_kernel(x_ref, od_ref, om_ref, wd_ref, wm_ref, gpost_ref, gpre_ref,
                     x1_ref, h2_ref):
    mixed = (jnp.dot(od_ref[...], wd_ref[...], preferred_element_type=F32)
             + jnp.dot(om_ref[...], wm_ref[...], preferred_element_type=F32))
    x1 = x_ref[...] + _rms(mixed, gpost_ref[...])
    x1_ref[...] = x1
    h2_ref[...] = _rms(x1, gpre_ref[...]).astype(BF16)


def _out_proj(x2, od, om, wd, wm, gpost, gpre):
    t, d = x2.shape
    tm = OUT_PROJ_ROWS
    row = lambda i: (i, 0)
    const = lambda i: (0, 0)
    return pl.pallas_call(
        _out_proj_kernel,
        grid=(t // tm,),
        in_specs=[
            pl.BlockSpec((tm, d), row),
            pl.BlockSpec((tm, od.shape[1]), row),
            pl.BlockSpec((tm, om.shape[1]), row),
            pl.BlockSpec(wd.shape, const),
            pl.BlockSpec(wm.shape, const),
            pl.BlockSpec((1, d), const),
            pl.BlockSpec((1, d), const),
        ],
        out_specs=[pl.BlockSpec((tm, d), row), pl.BlockSpec((tm, d), row)],
        out_shape=[jax.ShapeDtypeStruct((t, d), F32), jax.ShapeDtypeStruct((t, d), BF16)],
        compiler_params=_params(("arbitrary",)),
        name="out_proj",
    )(x2, od, om, wd, wm, gpost, gpre)


def _mlp_kernel(x1_ref, h2_ref, wu_ref, wd_ref, g_ref, o_ref):
    h = h2_ref[...]
    dff = wu_ref.shape[1]
    acc = None
    for f in range(dff // MLP_FF_CHUNK):
        sl = slice(f * MLP_FF_CHUNK, (f + 1) * MLP_FF_CHUNK)
        a = jnp.maximum(jnp.dot(h, wu_ref[:, sl], preferred_element_type=F32), 0.0)
        part = jnp.dot((a * a).astype(BF16), wd_ref[sl, :], preferred_element_type=F32)
        acc = part if acc is None else acc + part
    o_ref[...] = x1_ref[...] + _rms(acc, g_ref[...])


def _mlp(x1, h2, wu, wd, g):
    t, d = x1.shape
    tm = MLP_ROWS
    row = lambda i: (i, 0)
    const = lambda i: (0, 0)
    return pl.pallas_call(
        _mlp_kernel,
        grid=(t // tm,),
        in_specs=[
            pl.BlockSpec((tm, d), row),
            pl.BlockSpec((tm, d), row),
            pl.BlockSpec(wu.shape, const),
            pl.BlockSpec(wd.shape, const),
            pl.BlockSpec((1, d), const),
        ],
        out_specs=pl.BlockSpec((tm, d), row),
        out_shape=jax.ShapeDtypeStruct((t, d), F32),
        compiler_params=_params(("arbitrary",)),
        name="mlp",
    )(x1, h2, wu, wd, g)


def _rope_tables(seq):
    d = DIFF_HEAD_DIM
    inv = ROPE_THETA ** (-jnp.arange(0, d, 2, dtype=F32) / d)
    ang = jnp.arange(seq, dtype=jnp.int32).astype(F32)[:, None] * inv[None, :]
    cos = jnp.cos(ang)
    sin = jnp.sin(ang)
    zero = jnp.zeros_like(sin)
    reps = LANES // d
    cos_t = jnp.tile(jnp.concatenate([cos, cos], axis=1), (1, reps))
    slo_t = jnp.tile(jnp.concatenate([-sin, zero], axis=1), (1, reps))
    shi_t = jnp.tile(jnp.concatenate([zero, sin], axis=1), (1, reps))
    return cos_t, slo_t, shi_t


def _layer(x, l, norm_mix_pre, w_in, conv_w, conv_b, b_igate, b_fgate, lambda_q1,
           lambda_k1, lambda_q2, lambda_k2, diff_norm, mlstm_norm, w_out,
           norm_mix_post, norm_mlp_pre, w_up, w_down, norm_mlp_post):
    b, s, d = x.shape
    nh, dqk = N_MLSTM_HEADS, MLSTM_QK_DIM
    x2 = x.reshape(b * s, d)

    w = w_in[l]
    wq = w[:, 1536:1792].reshape(d, nh, dqk)
    wk = w[:, 1792:2048].reshape(d, nh, dqk)
    wqk = jnp.concatenate([wq, wk], axis=2).reshape(d, 2 * nh * dqk)
    wg = jnp.pad(w[:, 3072:3080], ((0, 0), (0, LANES - 2 * nh)))
    w_main = jnp.concatenate([w[:, 0:1536], wqk, w[:, 2048:3072], wg], axis=1).astype(BF16)

    def qk_interleave(v):
        lead = v.shape[:-1]
        q = v[..., :nh * dqk].reshape(lead + (nh, dqk))
        k = v[..., nh * dqk:].reshape(lead + (nh, dqk))
        return jnp.concatenate([q, k], axis=-1).reshape(lead + (2 * nh * dqk,))

    cw = qk_interleave(conv_w[l])
    cb = qk_interleave(conv_b[l])[None, :]
    qs = qk_interleave(jnp.concatenate([jnp.full((nh * dqk,), dqk ** -0.5, F32),
                                        jnp.ones((nh * dqk,), F32)]))[None, :]
    cos_t, slo_t, shi_t = _rope_tables(s)

    dq, dk, dv, mqk, mv, mo, gates = _in_proj(
        x2, norm_mix_pre[l][None, :], w_main, cos_t, slo_t, shi_t, cw, cb, qs,
        batch=b, seq=s)

    lam_p = jnp.stack([lambda_q1[l], lambda_k1[l], lambda_q2[l], lambda_k2[l]]).astype(F32)
    o_diff = _diff_attn(lam_p, diff_norm[l][:, None], dq.reshape(b, s, -1),
                        dk.reshape(b, s, -1), dv.reshape(b, s, -1),
                        lam_init=_lambda_init(l))

    bias = jnp.concatenate([b_igate[l], b_fgate[l]]).astype(F32)[:, None]
    o_mlstm = _mlstm(gates, bias, mqk.reshape(b, s, -1), mv.reshape(b, s, -1),
                     mo.reshape(b, s, -1), mlstm_norm[l].reshape(1, -1))

    wo = w_out[l].astype(BF16)
    nd = o_diff.shape[-1]
    x1, h2 = _out_proj(x2, o_diff.reshape(b * s, -1), o_mlstm.reshape(b * s, -1),
                       wo[:nd], wo[nd:], norm_mix_post[l][None, :],
                       norm_mlp_pre[l][None, :])
    out = _mlp(x1, h2, w_up[l].astype(BF16), w_down[l].astype(BF16),
               norm_mlp_post[l][None, :])
    return out.reshape(b, s, d)


def kernel(x, norm_mix_pre, w_in, conv_w, conv_b, b_igate, b_fgate, lambda_q1, lambda_k1,
           lambda_q2, lambda_k2, diff_norm, mlstm_norm, w_out, norm_mix_post,
           norm_mlp_pre, w_up, w_down, norm_mlp_post):
    for l in range(w_in.shape[0]):
        x = _layer(x, l, norm_mix_pre, w_in, conv_w, conv_b, b_igate, b_fgate,
                   lambda_q1, lambda_k1, lambda_q2, lambda_k2, diff_norm, mlstm_norm,
                   w_out, norm_mix_post, norm_mlp_pre, w_up, w_down, norm_mlp_post)
    return x
```

```python
import functools
import math

import numpy as np
import jax
import jax.numpy as jnp
from jax import lax
from jax.experimental import pallas as pl
from jax.experimental.pallas import tpu as pltpu

F32 = jnp.float32
BF16 = jnp.bfloat16

N_DIFF_HEADS = 4
DIFF_HEAD_DIM = 64
N_MLSTM_HEADS = 4
MLSTM_QK_DIM = 64
MLSTM_V_DIM = 128
CONV_WIDTH = 4
ROPE_THETA = 10000.0
EPS = 1e-6

LANES = 128
SUBLANES = 8
VMEM_LIMIT_BYTES = 56 * 1024 * 1024

IN_PROJ_ROWS = 512
LOG2E = math.log2(math.e)
ATTN_ONES_ROWS = 16
ATTN_Q_ROWS = 256
ATTN_KV_ROWS = 256
ATTN_COL_CHUNK = 2048
MLSTM_CHUNK = 128
OUT_PROJ_ROWS = 512
MLP_ROWS = 512
MLP_FF_CHUNK = 1024


def _lambda_init(layer):
    return 0.8 - 0.6 * math.exp(-0.3 * layer)


def _params(semantics):
    return pltpu.CompilerParams(dimension_semantics=semantics,
                                vmem_limit_bytes=VMEM_LIMIT_BYTES)


def _rms(x, g):
    return x * lax.rsqrt(jnp.mean(x * x, axis=-1, keepdims=True) + EPS) * g


def _in_proj_kernel(x_ref, g_ref, w_ref, cos_ref, slo_ref, shi_ref, cw_ref, cb_ref,
                    qs_ref, dq_ref, dk_ref, dv_ref, mqk_ref, mv_ref, mo_ref,
                    gate_ref, carry_ref, *, tiles_per_seq):
    tm = x_ref.shape[0]
    i = pl.program_id(0)
    h = _rms(x_ref[...], g_ref[...]).astype(BF16)

    def proj(lo, hi):
        return jnp.dot(h, w_ref[:, lo:hi], preferred_element_type=F32)

    cos = cos_ref[...]
    s_lo = slo_ref[...]
    s_hi = shi_ref[...]

    def rope(p):
        outs = []
        for gi in range(p.shape[1] // LANES):
            v = p[:, gi * LANES:(gi + 1) * LANES]
            outs.append(v * cos + pltpu.roll(v, LANES - 32, 1) * s_lo
                        + pltpu.roll(v, 32, 1) * s_hi)
        return jnp.concatenate(outs, axis=1)

    dq_ref[...] = (rope(proj(0, 512)) * (DIFF_HEAD_DIM ** -0.5 * LOG2E)).astype(BF16)
    dk_ref[...] = rope(proj(512, 1024)).astype(BF16)
    dv_ref[...] = proj(1024, 1536).astype(BF16)

    pre = proj(1536, 2048)

    @pl.when(i % tiles_per_seq == 0)
    def _():
        carry_ref[0:SUBLANES, :] = jnp.zeros((SUBLANES, pre.shape[1]), F32)

    carry_ref[SUBLANES:SUBLANES + tm, :] = pre
    cw = cw_ref[...]
    y = cb_ref[...] + cw[3:4, :] * pre
    for j in range(1, CONV_WIDTH):
        y = y + cw[3 - j:4 - j, :] * carry_ref[SUBLANES - j:SUBLANES - j + tm, :]
    carry_ref[0:SUBLANES, :] = pre[tm - SUBLANES:tm, :]
    mqk_ref[...] = y * jax.nn.sigmoid(y) * qs_ref[...]

    mv_ref[...] = proj(2048, 2560).astype(BF16)
    mo_ref[...] = proj(2560, 3072).astype(BF16)
    gate_ref[0] = proj(3072, 3200).T[0:SUBLANES, :]


def _in_proj(x2, g, w_main, cos_t, slo_t, shi_t, cw, cb, qs, *, batch, seq):
    t, d = x2.shape
    tm = IN_PROJ_ROWS
    tps = seq // tm
    nw = w_main.shape[1]
    row = lambda i: (i, 0)
    const = lambda i: (0, 0)
    pos = lambda i: (i % tps, 0)
    out_bf = jax.ShapeDtypeStruct((t, 512), BF16)
    return pl.pallas_call(
        functools.partial(_in_proj_kernel, tiles_per_seq=tps),
        grid=(t // tm,),
        in_specs=[
            pl.BlockSpec((tm, d), row),
            pl.BlockSpec((1, d), const),
            pl.BlockSpec((d, nw), const),
            pl.BlockSpec((tm, LANES), pos),
            pl.BlockSpec((tm, LANES), pos),
            pl.BlockSpec((tm, LANES), pos),
            pl.BlockSpec((CONV_WIDTH, 512), const),
            pl.BlockSpec((1, 512), const),
            pl.BlockSpec((1, 512), const),
        ],
        out_specs=[
            pl.BlockSpec((tm, 512), row),
            pl.BlockSpec((tm, 512), row),
            pl.BlockSpec((tm, 512), row),
            pl.BlockSpec((tm, 512), row),
            pl.BlockSpec((tm, 512), row),
            pl.BlockSpec((tm, 512), row),
            pl.BlockSpec((1, SUBLANES, tm), lambda i: (i // tps, 0, i % tps)),
        ],
        out_shape=[out_bf, out_bf, out_bf,
                   jax.ShapeDtypeStruct((t, 512), F32), out_bf, out_bf,
                   jax.ShapeDtypeStruct((batch, SUBLANES, seq), F32)],
        scratch_shapes=[pltpu.VMEM((tm + SUBLANES, 512), F32)],
        compiler_params=_params(("arbitrary",)),
        name="in_proj",
    )(x2, g, w_main, cos_t, slo_t, shi_t, cw, cb, qs)


def _diff_attn_kernel(lam_ref, gn_ref, q_ref, k_ref, v_ref, o_ref,
                      acc_ref, m_ref, *, lam_init):
    s = q_ref.shape[1]
    tk = ATTN_KV_ROWS
    tf = ATTN_Q_ROWS
    lp = lam_ref[...]
    lam = (jnp.exp(jnp.sum(lp[0:1] * lp[1:2], axis=-1, keepdims=True))
           - jnp.exp(jnp.sum(lp[2:3] * lp[3:4], axis=-1, keepdims=True)) + lam_init)

    lane = lax.broadcasted_iota(jnp.int32, (1, LANES), 1)
    nt = (((1,), (1,)), ((), ()))
    keep = (lax.broadcasted_iota(jnp.int32, (tk, tk), 1)
            >= lax.broadcasted_iota(jnp.int32, (tk, tk), 0))
    q = q_ref[0]
    zero = jnp.zeros_like(q)
    qmaps = (jnp.where(lane < DIFF_HEAD_DIM, q, zero),
             jnp.where(lane >= DIFF_HEAD_DIM, q, zero))

    m_ref[...] = jnp.full(m_ref.shape, -jnp.inf, F32)
    acc_ref[...] = jnp.zeros(acc_ref.shape, F32)
    ones_rows = jnp.ones((ATTN_ONES_ROWS, tk), BF16)

    cw = ATTN_COL_CHUNK
    units = [(j, mi, c0, min(c0 + cw, s))
             for j in range(s // tk) for c0 in range(j * tk, s, cw) for mi in range(2)]

    def scores(unit):
        j, mi, c0, c1 = unit
        kb = k_ref[0, j * tk:(j + 1) * tk, :]
        st = lax.dot_general(kb, qmaps[mi][c0:c1, :], nt, preferred_element_type=F32)
        if c0 == j * tk:
            diag = jnp.where(keep, st[:, :tk], -jnp.inf)
            st = diag if c1 - c0 == tk else jnp.concatenate([diag, st[:, tk:]], axis=1)
        return st

    vts = {}

    def vt_aug(j):
        if j not in vts:
            lo = j * tk
            vts[j] = jnp.concatenate(
                [jnp.concatenate(
                    [v_ref[0, lo + c * LANES:lo + (c + 1) * LANES, :].astype(F32).T
                     for c in range(tk // LANES)], axis=1).astype(BF16), ones_rows], axis=0)
        return vts[j]

    st_next = scores(units[0])
    for u, (j, mi, c0, c1) in enumerate(units):
        st = st_next
        if u + 1 < len(units):
            st_next = scores(units[u + 1])
        cols = slice(mi * s + c0, mi * s + c1)
        m_old = m_ref[:, cols]
        m_new = jnp.maximum(m_old, jnp.max(st, axis=0, keepdims=True))
        alpha = jnp.exp2(m_old - m_new)
        p = jnp.exp2(st - m_new).astype(BF16)
        m_ref[:, cols] = m_new
        acc_ref[:, cols] = alpha * acc_ref[:, cols] + jnp.dot(
            vt_aug(j), p, preferred_element_type=F32)

    for c0 in range(0, s, tf):
        a1 = acc_ref[0:LANES, c0:c0 + tf] / acc_ref[LANES:LANES + 1, c0:c0 + tf]
        a2 = (acc_ref[0:LANES, s + c0:s + c0 + tf]
              / acc_ref[LANES:LANES + 1, s + c0:s + c0 + tf])
        ot = a1 - lam * a2
        ms = jnp.mean(ot * ot, axis=0, keepdims=True)
        on = ot * lax.rsqrt(ms + EPS) * gn_ref[...] * (1.0 - lam_init)
        o_ref[0, c0:c0 + tf, :] = on.T.astype(o_ref.dtype)


def _diff_attn(lam_p, gn_col, dq, dk, dv, *, lam_init):
    b, s, w = dq.shape
    nh = w // LANES
    head = lambda bi, hi: (bi, 0, hi)
    const = lambda bi, hi: (0, 0)
    return pl.pallas_call(
        functools.partial(_diff_attn_kernel, lam_init=lam_init),
        grid=(b, nh),
        in_specs=[
            pl.BlockSpec((4, DIFF_HEAD_DIM), const),
            pl.BlockSpec((LANES, 1), const),
            pl.BlockSpec((1, s, LANES), head),
            pl.BlockSpec((1, s, LANES), head),
            pl.BlockSpec((1, s, LANES), head),
        ],
        out_specs=pl.BlockSpec((1, s, LANES), head),
        out_shape=jax.ShapeDtypeStruct((b, s, w), BF16),
        scratch_shapes=[
            pltpu.VMEM((LANES + ATTN_ONES_ROWS, 2 * s), F32),
            pltpu.VMEM((1, 2 * s), F32),
        ],
        compiler_params=_params(("arbitrary", "arbitrary")),
        name="diff_attn",
    )(lam_p, gn_col, dq, dk, dv)


def _seg_scan(x, seg_off, seg_len, op, fill, reverse=False):
    n = x.shape[1]
    d = 1
    while d < seg_len:
        if reverse:
            sh = pltpu.roll(x, n - d, 1)
            ok = seg_off < seg_len - d
        else:
            sh = pltpu.roll(x, d, 1)
            ok = seg_off >= d
        x = op(x, jnp.where(ok, sh, fill))
        d *= 2
    return x


def _log_sigmoid(x):
    return -(jnp.maximum(-x, 0.0) + jnp.log1p(jnp.exp(-jnp.abs(x))))


def _mlstm_kernel(gate_ref, bias_ref, qk_ref, v_ref, o_in_ref, gn_ref, out_ref,
                  arow_ref, rowbuf_ref, colbuf_ref, c_ref):
    s = qk_ref.shape[1]
    lc = MLSTM_CHUNK
    nc = s // lc
    nh = N_MLSTM_HEADS

    g = gate_ref[0] + bias_ref[...]
    i_pre = g
    logf = _log_sigmoid(pltpu.roll(g, nh, 0))
    pos = lax.broadcasted_iota(jnp.int32, (1, s), 1)
    seg = pos % lc
    last = seg == lc - 1
    add = lambda a, b_: a + b_
    bcum = _seg_scan(logf, seg, lc, add, 0.0)
    a = i_pre - bcum
    cmax = _seg_scan(a, seg, lc, jnp.maximum, -jnp.inf)
    gb = _seg_scan(jnp.where(last, bcum, 0.0), seg, lc, add, 0.0, reverse=True)
    xb = gb + _seg_scan(jnp.where(last, cmax, -jnp.inf), seg, lc, jnp.maximum,
                        -jnp.inf, reverse=True)
    m_prev = jnp.zeros((SUBLANES, lc), F32)
    mp, mc = [], []
    for c in range(nc):
        m_cur = jnp.maximum(gb[:, c * lc:(c + 1) * lc] + m_prev, xb[:, c * lc:(c + 1) * lc])
        mp.append(m_prev)
        mc.append(m_cur)
        m_prev = m_cur
    mprev = jnp.concatenate(mp, axis=1)
    mcur = jnp.concatenate(mc, axis=1)
    u = jnp.maximum(mprev, cmax)
    rowbuf_ref[0:8, :] = u
    rowbuf_ref[8:16, :] = jnp.exp(mprev - u)
    rowbuf_ref[16:24, :] = jnp.exp(-(u + bcum))
    rowbuf_ref[24:32, :] = jnp.exp(gb + a - mcur)
    rowbuf_ref[32:40, :] = jnp.exp(gb + mprev - mcur)
    rowbuf_ref[40:LANES, :] = jnp.zeros((LANES - 40, s), F32)
    for c in range(nc):
        arow_ref[c] = a[:, c * lc:(c + 1) * lc]
        colbuf_ref[c * lc:(c + 1) * lc, :] = rowbuf_ref[:, c * lc:(c + 1) * lc].T

    c_ref[...] = jnp.zeros(c_ref.shape, F32)
    lane = lax.broadcasted_iota(jnp.int32, (1, LANES), 1)
    ones_col = jnp.where(lane == 0, 1.0, 0.0).astype(BF16)
    r = lax.broadcasted_iota(jnp.int32, (lc, lc), 0)
    cc = lax.broadcasted_iota(jnp.int32, (lc, lc), 1)
    tril = r >= cc
    nt = (((1,), (1,)), ((), ()))
    tn = (((0,), (0,)), ((), ()))
    gn = gn_ref[...]

    def chunk(c, _):
        start = pl.multiple_of(c * lc, lc)
        cols = colbuf_ref[pl.ds(start, lc), :]
        arow = arow_ref[c]
        for hd in range(nh):
            sl = slice(hd * LANES, (hd + 1) * LANES)
            qk = qk_ref[0, pl.ds(start, lc), sl]
            kr = pltpu.roll(qk, MLSTM_QK_DIM, 1)
            qm = jnp.where(lane < MLSTM_QK_DIM, qk, 0.0).astype(BF16)
            vaug = jnp.concatenate(
                [v_ref[0, pl.ds(start, lc), sl], jnp.broadcast_to(ones_col, (lc, LANES))],
                axis=1)
            sqk = lax.dot_general(qm, kr.astype(BF16), nt, preferred_element_type=F32)
            arg = jnp.where(tril, arow[hd:hd + 1, :] - cols[:, hd:hd + 1], -jnp.inf)
            p = (sqk * jnp.exp(arg)).astype(BF16)
            cst = c_ref[hd]
            nd = (cols[:, 8 + hd:9 + hd]
                  * jnp.dot(qm, cst.astype(BF16), preferred_element_type=F32)
                  + jnp.dot(p, vaug, preferred_element_type=F32))
            den = jnp.maximum(jnp.abs(nd[:, LANES:LANES + 1]), cols[:, 16 + hd:17 + hd])
            hv = nd[:, 0:LANES] / den
            hn = _rms(hv, gn[:, sl])
            og = o_in_ref[0, pl.ds(start, lc), sl].astype(F32)
            out_ref[0, pl.ds(start, lc), sl] = (hn * jax.nn.sigmoid(og)).astype(out_ref.dtype)
            kw = (kr * cols[:, 24 + hd:25 + hd]).astype(BF16)
            c_ref[hd] = (cols[0:1, 32 + hd:33 + hd] * cst
                         + lax.dot_general(kw, vaug, tn, preferred_element_type=F32))
        return 0

    lax.fori_loop(0, nc, chunk, 0)


def _mlstm(gates, bias, mqk, mv, mo, gn):
    b, s, w = mqk.shape
    lc = MLSTM_CHUNK
    seq = lambda bi: (bi, 0, 0)
    return pl.pallas_call(
        _mlstm_kernel,
        grid=(b,),
        in_specs=[
            pl.BlockSpec((1, SUBLANES, s), seq),
            pl.BlockSpec((SUBLANES, 1), lambda bi: (0, 0)),
            pl.BlockSpec((1, s, w), seq),
            pl.BlockSpec((1, s, w), seq),
            pl.BlockSpec((1, s, w), seq),
            pl.BlockSpec((1, w), lambda bi: (0, 0)),
        ],
        out_specs=pl.BlockSpec((1, s, w), seq),
        out_shape=jax.ShapeDtypeStruct((b, s, w), BF16),
        scratch_shapes=[
            pltpu.VMEM((s // lc, SUBLANES, lc), F32),
            pltpu.VMEM((LANES, s), F32),
            pltpu.VMEM((s, LANES), F32),
            pltpu.VMEM((N_MLSTM_HEADS, LANES, 2 * LANES), F32),
        ],
        compiler_params=_params(("arbitrary",)),
        name="mlstm",
    )(gates, bias, mqk, mv, mo, gn)


def _out_proj_kernel(x_ref, od_ref, om_ref, wd_ref, wm_ref, gpost_ref, gpre_ref,
                     x1_ref, h2_ref):
    mixed = (jnp.dot(od_ref[...], wd_ref[...], preferred_element_type=F32)
             + jnp.dot(om_ref[...], wm_ref[...], preferred_element_type=F32))
    x1 = x_ref[...] + _rms(mixed, gpost_ref[...])
    x1_ref[...] = x1
    h2_ref[...] = _rms(x1, gpre_ref[...]).astype(BF16)


def _out_proj(x2, od, om, wd, wm, gpost, gpre):
    t, d = x2.shape
    tm = OUT_PROJ_ROWS
    row = lambda i: (i, 0)
    const = lambda i: (0, 0)
    return pl.pallas_call(
        _out_proj_kernel,
        grid=(t // tm,),
        in_specs=[
            pl.BlockSpec((tm, d), row),
            pl.BlockSpec((tm, od.shape[1]), row),
            pl.BlockSpec((tm, om.shape[1]), row),
            pl.BlockSpec(wd.shape, const),
            pl.BlockSpec(wm.shape, const),
            pl.BlockSpec((1, d), const),
            pl.BlockSpec((1, d), const),
        ],
        out_specs=[pl.BlockSpec((tm, d), row), pl.BlockSpec((tm, d), row)],
        out_shape=[jax.ShapeDtypeStruct((t, d), F32), jax.ShapeDtypeStruct((t, d), BF16)],
        compiler_params=_params(("arbitrary",)),
        name="out_proj",
    )(x2, od, om, wd, wm, gpost, gpre)


def _mlp_kernel(x1_ref, h2_ref, wu_ref, wd_ref, g_ref, o_ref):
    h = h2_ref[...]
    dff = wu_ref.shape[1]
    acc = None
    for f in range(dff // MLP_FF_CHUNK):
        sl = slice(f * MLP_FF_CHUNK, (f + 1) * MLP_FF_CHUNK)
        a = jnp.maximum(jnp.dot(h, wu_ref[:, sl], preferred_element_type=F32), 0.0)
        part = jnp.dot((a * a).astype(BF16), wd_ref[sl, :], preferred_element_type=F32)
        acc = part if acc is None else acc + part
    o_ref[...] = x1_ref[...] + _rms(acc, g_ref[...])


def _mlp(x1, h2, wu, wd, g):
    t, d = x1.shape
    tm = MLP_ROWS
    row = lambda i: (i, 0)
    const = lambda i: (0, 0)
    return pl.pallas_call(
        _mlp_kernel,
        grid=(t // tm,),
        in_specs=[
            pl.BlockSpec((tm, d), row),
            pl.BlockSpec((tm, d), row),
            pl.BlockSpec(wu.shape, const),
            pl.BlockSpec(wd.shape, const),
            pl.BlockSpec((1, d), const),
        ],
        out_specs=pl.BlockSpec((tm, d), row),
        out_shape=jax.ShapeDtypeStruct((t, d), F32),
        compiler_params=_params(("arbitrary",)),
        name="mlp",
    )(x1, h2, wu, wd, g)


def _rope_tables(seq):
    d = DIFF_HEAD_DIM
    inv = ROPE_THETA ** (-jnp.arange(0, d, 2, dtype=F32) / d)
    ang = jnp.arange(seq, dtype=jnp.int32).astype(F32)[:, None] * inv[None, :]
    cos = jnp.cos(ang)
    sin = jnp.sin(ang)
    zero = jnp.zeros_like(sin)
    reps = LANES // d
    cos_t = jnp.tile(jnp.concatenate([cos, cos], axis=1), (1, reps))
    slo_t = jnp.tile(jnp.concatenate([-sin, zero], axis=1), (1, reps))
    shi_t = jnp.tile(jnp.concatenate([zero, sin], axis=1), (1, reps))
    return cos_t, slo_t, shi_t


def _layer(x, l, norm_mix_pre, w_in, conv_w, conv_b, b_igate, b_fgate, lambda_q1,
           lambda_k1, lambda_q2, lambda_k2, diff_norm, mlstm_norm, w_out,
           norm_mix_post, norm_mlp_pre, w_up, w_down, norm_mlp_post):
    b, s, d = x.shape
    nh, dqk = N_MLSTM_HEADS, MLSTM_QK_DIM
    x2 = x.reshape(b * s, d)

    w = w_in[l]
    wq = w[:, 1536:1792].reshape(d, nh, dqk)
    wk = w[:, 1792:2048].reshape(d, nh, dqk)
    wqk = jnp.concatenate([wq, wk], axis=2).reshape(d, 2 * nh * dqk)
    wg = jnp.pad(w[:, 3072:3080], ((0, 0), (0, LANES - 2 * nh)))
    w_main = jnp.concatenate([w[:, 0:1536], wqk, w[:, 2048:3072], wg], axis=1).astype(BF16)

    def qk_interleave(v):
        lead = v.shape[:-1]
        q = v[..., :nh * dqk].reshape(lead + (nh, dqk))
        k = v[..., nh * dqk:].reshape(lead + (nh, dqk))
        return jnp.concatenate([q, k], axis=-1).reshape(lead + (2 * nh * dqk,))

    cw = qk_interleave(conv_w[l])
    cb = qk_interleave(conv_b[l])[None, :]
    qs = qk_interleave(jnp.concatenate([jnp.full((nh * dqk,), dqk ** -0.5, F32),
                                        jnp.ones((nh * dqk,), F32)]))[None, :]
    cos_t, slo_t, shi_t = _rope_tables(s)

    dq, dk, dv, mqk, mv, mo, gates = _in_proj(
        x2, norm_mix_pre[l][None, :], w_main, cos_t, slo_t, shi_t, cw, cb, qs,
        batch=b, seq=s)

    lam_p = jnp.stack([lambda_q1[l], lambda_k1[l], lambda_q2[l], lambda_k2[l]]).astype(F32)
    o_diff = _diff_attn(lam_p, diff_norm[l][:, None], dq.reshape(b, s, -1),
                        dk.reshape(b, s, -1), dv.reshape(b, s, -1),
                        lam_init=_lambda_init(l))

    bias = jnp.concatenate([b_igate[l], b_fgate[l]]).astype(F32)[:, None]
    o_mlstm = _mlstm(gates, bias, mqk.reshape(b, s, -1), mv.reshape(b, s, -1),
                     mo.reshape(b, s, -1), mlstm_norm[l].reshape(1, -1))

    wo = w_out[l].astype(BF16)
    nd = o_diff.shape[-1]
    x1, h2 = _out_proj(x2, o_diff.reshape(b * s, -1), o_mlstm.reshape(b * s, -1),
                       wo[:nd], wo[nd:], norm_mix_post[l][None, :],
                       norm_mlp_pre[l][None, :])
    out = _mlp(x1, h2, w_up[l].astype(BF16), w_down[l].astype(BF16),
               norm_mlp_post[l][None, :])
    return out.reshape(b, s, d)


def kernel(x, norm_mix_pre, w_in, conv_w, conv_b, b_igate, b_fgate, lambda_q1, lambda_k1,
           lambda_q2, lambda_k2, diff_norm, mlstm_norm, w_out, norm_mix_post,
           norm_mlp_pre, w_up, w_down, norm_mlp_post):
    for l in range(w_in.shape[0]):
        x = _layer(x, l, norm_mix_pre, w_in, conv_w, conv_b, b_igate, b_fgate,
                   lambda_q1, lambda_k1, lambda_q2, lambda_k2, diff_norm, mlstm_norm,
                   w_out, norm_mix_post, norm_mlp_pre, w_up, w_down, norm_mlp_post)
    return x
```

```python
import functools
import math

import numpy as np
import jax
import jax.numpy as jnp
from jax import lax
from jax.experimental import pallas as pl
from jax.experimental.pallas import tpu as pltpu

F32 = jnp.float32
BF16 = jnp.bfloat16

N_DIFF_HEADS = 4
DIFF_HEAD_DIM = 64
N_MLSTM_HEADS = 4
MLSTM_QK_DIM = 64
MLSTM_V_DIM = 128
CONV_WIDTH = 4
ROPE_THETA = 10000.0
EPS = 1e-6

LANES = 128
SUBLANES = 8
VMEM_LIMIT_BYTES = 56 * 1024 * 1024

IN_PROJ_ROWS = 512
LOG2E = math.log2(math.e)
ATTN_ONES_ROWS = 16
ATTN_Q_ROWS = 256
ATTN_KV_ROWS = 256
ATTN_COL_CHUNK = 2048
MLSTM_CHUNK = 128
MLP_ROWS = 512
MLP_FF_CHUNK = 1024


def _lambda_init(layer):
    return 0.8 - 0.6 * math.exp(-0.3 * layer)


def _params(semantics):
    return pltpu.CompilerParams(dimension_semantics=semantics,
                                vmem_limit_bytes=VMEM_LIMIT_BYTES)


def _rms(x, g):
    return x * lax.rsqrt(jnp.mean(x * x, axis=-1, keepdims=True) + EPS) * g


def _in_proj_kernel(x_ref, g_ref, w_ref, cos_ref, slo_ref, shi_ref, cw_ref, cb_ref,
                    qs_ref, dq_ref, dk_ref, dv_ref, mqk_ref, mv_ref, mo_ref,
                    gate_ref, carry_ref, *, tiles_per_seq):
    tm = x_ref.shape[0]
    i = pl.program_id(0)

    @pl.when(i % tiles_per_seq == 0)
    def _():
        carry_ref[...] = jnp.zeros(carry_ref.shape, F32)

    h = _rms(x_ref[...], g_ref[...]).astype(BF16)

    def proj(lo, hi):
        return jnp.dot(h, w_ref[:, lo:hi], preferred_element_type=F32)

    pre = proj(1536, 2048)
    tail = carry_ref[...]
    carry_ref[...] = pre[tm - SUBLANES:tm, :]
    rows = lax.broadcasted_iota(jnp.int32, (SUBLANES, 1), 0)
    cw = cw_ref[...]
    y = cb_ref[...] + cw[3:4, :] * pre
    for j in range(1, CONV_WIDTH):
        sh = pltpu.roll(pre, j, 0)
        head = jnp.where(rows < j, pltpu.roll(tail, j, 0), sh[0:SUBLANES, :])
        sh = jnp.concatenate([head, sh[SUBLANES:, :]], axis=0)
        y = y + cw[3 - j:4 - j, :] * sh
    mqk_ref[...] = y * jax.nn.sigmoid(y) * qs_ref[...]

    gate_ref[0] = proj(3072, 3200).T[0:SUBLANES, :]

    cos = cos_ref[...]
    s_lo = slo_ref[...]
    s_hi = shi_ref[...]

    def rope(p):
        outs = []
        for gi in range(p.shape[1] // LANES):
            v = p[:, gi * LANES:(gi + 1) * LANES]
            outs.append(v * cos + pltpu.roll(v, LANES - 32, 1) * s_lo
                        + pltpu.roll(v, 32, 1) * s_hi)
        return jnp.concatenate(outs, axis=1)

    dq_ref[...] = (rope(proj(0, 512)) * (DIFF_HEAD_DIM ** -0.5 * LOG2E)).astype(BF16)
    dk_ref[...] = rope(proj(512, 1024)).astype(BF16)
    dv_ref[...] = proj(1024, 1536).astype(BF16)
    mv_ref[...] = proj(2048, 2560).astype(BF16)
    mo_ref[...] = proj(2560, 3072).astype(BF16)


def _in_proj(x2, g, w_main, cos_t, slo_t, shi_t, cw, cb, qs, *, batch, seq):
    t, d = x2.shape
    tm = IN_PROJ_ROWS
    tps = seq // tm
    nw = w_main.shape[1]
    row = lambda i: (i, 0)
    const = lambda i: (0, 0)
    pos = lambda i: (i % tps, 0)
    out_bf = jax.ShapeDtypeStruct((t, 512), BF16)
    return pl.pallas_call(
        functools.partial(_in_proj_kernel, tiles_per_seq=tps),
        grid=(t // tm,),
        in_specs=[
            pl.BlockSpec((tm, d), row),
            pl.BlockSpec((1, d), const),
            pl.BlockSpec((d, nw), const),
            pl.BlockSpec((tm, LANES), pos),
            pl.BlockSpec((tm, LANES), pos),
            pl.BlockSpec((tm, LANES), pos),
            pl.BlockSpec((CONV_WIDTH, 512), const),
            pl.BlockSpec((1, 512), const),
            pl.BlockSpec((1, 512), const),
        ],
        out_specs=[
            pl.BlockSpec((tm, 512), row),
            pl.BlockSpec((tm, 512), row),
            pl.BlockSpec((tm, 512), row),
            pl.BlockSpec((tm, 512), row),
            pl.BlockSpec((tm, 512), row),
            pl.BlockSpec((tm, 512), row),
            pl.BlockSpec((1, SUBLANES, tm), lambda i: (i // tps, 0, i % tps)),
        ],
        out_shape=[out_bf, out_bf, out_bf,
                   jax.ShapeDtypeStruct((t, 512), F32), out_bf, out_bf,
                   jax.ShapeDtypeStruct((batch, SUBLANES, seq), F32)],
        scratch_shapes=[pltpu.VMEM((SUBLANES, 512), F32)],
        compiler_params=_params(("arbitrary",)),
        name="in_proj",
    )(x2, g, w_main, cos_t, slo_t, shi_t, cw, cb, qs)


def _diff_attn_kernel(lam_ref, gn_ref, q_ref, k_ref, v_ref, o_ref,
                      acc_ref, m_ref, *, lam_init):
    s = q_ref.shape[1]
    tk = ATTN_KV_ROWS
    tf = ATTN_Q_ROWS
    lp = lam_ref[...]
    lam = (jnp.exp(jnp.sum(lp[0:1] * lp[1:2], axis=-1, keepdims=True))
           - jnp.exp(jnp.sum(lp[2:3] * lp[3:4], axis=-1, keepdims=True)) + lam_init)

    lane = lax.broadcasted_iota(jnp.int32, (1, LANES), 1)
    nt = (((1,), (1,)), ((), ()))
    keep = (lax.broadcasted_iota(jnp.int32, (tk, tk), 1)
            >= lax.broadcasted_iota(jnp.int32, (tk, tk), 0))
    q = q_ref[0]
    zero = jnp.zeros_like(q)
    qmaps = (jnp.where(lane < DIFF_HEAD_DIM, q, zero),
             jnp.where(lane >= DIFF_HEAD_DIM, q, zero))

    m_ref[...] = jnp.full(m_ref.shape, -jnp.inf, F32)
    acc_ref[...] = jnp.zeros(acc_ref.shape, F32)
    ones_rows = jnp.ones((ATTN_ONES_ROWS, tk), BF16)

    cw = ATTN_COL_CHUNK
    units = [(j, mi, c0, min(c0 + cw, s))
             for j in range(s // tk) for c0 in range(j * tk, s, cw) for mi in range(2)]

    def scores(unit):
        j, mi, c0, c1 = unit
        kb = k_ref[0, j * tk:(j + 1) * tk, :]
        st = lax.dot_general(kb, qmaps[mi][c0:c1, :], nt, preferred_element_type=F32)
        if c0 == j * tk:
            diag = jnp.where(keep, st[:, :tk], -jnp.inf)
            st = diag if c1 - c0 == tk else jnp.concatenate([diag, st[:, tk:]], axis=1)
        return st

    vts = {}

    def vt_aug(j):
        if j not in vts:
            lo = j * tk
            vts[j] = jnp.concatenate(
                [jnp.concatenate(
                    [v_ref[0, lo + c * LANES:lo + (c + 1) * LANES, :].astype(F32).T
                     for c in range(tk // LANES)], axis=1).astype(BF16), ones_rows], axis=0)
        return vts[j]

    st_next = scores(units[0])
    for u, (j, mi, c0, c1) in enumerate(units):
        st = st_next
        if u + 1 < len(units):
            st_next = scores(units[u + 1])
        cols = slice(mi * s + c0, mi * s + c1)
        m_old = m_ref[:, cols]
        m_new = jnp.maximum(m_old, jnp.max(st, axis=0, keepdims=True))
        alpha = jnp.exp2(m_old - m_new)
        p = jnp.exp2(st - m_new).astype(BF16)
        m_ref[:, cols] = m_new
        acc_ref[:, cols] = alpha * acc_ref[:, cols] + jnp.dot(
            vt_aug(j), p, preferred_element_type=F32)

    for c0 in range(0, s, tf):
        a1 = acc_ref[0:LANES, c0:c0 + tf] / acc_ref[LANES:LANES + 1, c0:c0 + tf]
        a2 = (acc_ref[0:LANES, s + c0:s + c0 + tf]
              / acc_ref[LANES:LANES + 1, s + c0:s + c0 + tf])
        ot = a1 - lam * a2
        ms = jnp.mean(ot * ot, axis=0, keepdims=True)
        on = ot * lax.rsqrt(ms + EPS) * gn_ref[...] * (1.0 - lam_init)
        o_ref[0, c0:c0 + tf, :] = on.T.astype(o_ref.dtype)


def _diff_attn(lam_p, gn_col, dq, dk, dv, *, lam_init):
    b, s, w = dq.shape
    nh = w // LANES
    head = lambda bi, hi: (bi, 0, hi)
    const = lambda bi, hi: (0, 0)
    return pl.pallas_call(
        functools.partial(_diff_attn_kernel, lam_init=lam_init),
        grid=(b, nh),
        in_specs=[
            pl.BlockSpec((4, DIFF_HEAD_DIM), const),
            pl.BlockSpec((LANES, 1), const),
            pl.BlockSpec((1, s, LANES), head),
            pl.BlockSpec((1, s, LANES), head),
            pl.BlockSpec((1, s, LANES), head),
        ],
        out_specs=pl.BlockSpec((1, s, LANES), head),
        out_shape=jax.ShapeDtypeStruct((b, s, w), BF16),
        scratch_shapes=[
            pltpu.VMEM((LANES + ATTN_ONES_ROWS, 2 * s), F32),
            pltpu.VMEM((1, 2 * s), F32),
        ],
        compiler_params=_params(("arbitrary", "arbitrary")),
        name="diff_attn",
    )(lam_p, gn_col, dq, dk, dv)


def _seg_scan(x, seg_off, seg_len, op, fill, reverse=False):
    n = x.shape[1]
    d = 1
    while d < seg_len:
        if reverse:
            sh = pltpu.roll(x, n - d, 1)
            ok = seg_off < seg_len - d
        else:
            sh = pltpu.roll(x, d, 1)
            ok = seg_off >= d
        x = op(x, jnp.where(ok, sh, fill))
        d *= 2
    return x


def _log_sigmoid(x):
    return -(jnp.maximum(-x, 0.0) + jnp.log1p(jnp.exp(-jnp.abs(x))))


def _mlstm_kernel(gate_ref, bias_ref, qk_ref, v_ref, o_in_ref, gn_ref, out_ref,
                  arow_ref, rowbuf_ref, colbuf_ref, c_ref):
    s = qk_ref.shape[1]
    lc = MLSTM_CHUNK
    nc = s // lc
    nh = N_MLSTM_HEADS

    g = gate_ref[0] + bias_ref[...]
    i_pre = g
    logf = _log_sigmoid(pltpu.roll(g, nh, 0))
    pos = lax.broadcasted_iota(jnp.int32, (1, s), 1)
    seg = pos % lc
    last = seg == lc - 1
    add = lambda a, b_: a + b_
    bcum = _seg_scan(logf, seg, lc, add, 0.0)
    a = i_pre - bcum
    cmax = _seg_scan(a, seg, lc, jnp.maximum, -jnp.inf)
    gb = _seg_scan(jnp.where(last, bcum, 0.0), seg, lc, add, 0.0, reverse=True)
    xb = gb + _seg_scan(jnp.where(last, cmax, -jnp.inf), seg, lc, jnp.maximum,
                        -jnp.inf, reverse=True)
    m_prev = jnp.zeros((SUBLANES, lc), F32)
    mp, mc = [], []
    for c in range(nc):
        m_cur = jnp.maximum(gb[:, c * lc:(c + 1) * lc] + m_prev, xb[:, c * lc:(c + 1) * lc])
        mp.append(m_prev)
        mc.append(m_cur)
        m_prev = m_cur
    mprev = jnp.concatenate(mp, axis=1)
    mcur = jnp.concatenate(mc, axis=1)
    u = jnp.maximum(mprev, cmax)
    rowbuf_ref[0:8, :] = u
    rowbuf_ref[8:16, :] = jnp.exp(mprev - u)
    rowbuf_ref[16:24, :] = jnp.exp(-(u + bcum))
    rowbuf_ref[24:32, :] = jnp.exp(gb + a - mcur)
    rowbuf_ref[32:40, :] = jnp.exp(gb + mprev - mcur)
    rowbuf_ref[40:LANES, :] = jnp.zeros((LANES - 40, s), F32)
    for c in range(nc):
        arow_ref[c] = a[:, c * lc:(c + 1) * lc]
        colbuf_ref[c * lc:(c + 1) * lc, :] = rowbuf_ref[:, c * lc:(c + 1) * lc].T

    c_ref[...] = jnp.zeros(c_ref.shape, F32)
    lane = lax.broadcasted_iota(jnp.int32, (1, LANES), 1)
    ones_col = jnp.where(lane == 0, 1.0, 0.0).astype(BF16)
    r = lax.broadcasted_iota(jnp.int32, (lc, lc), 0)
    cc = lax.broadcasted_iota(jnp.int32, (lc, lc), 1)
    tril = r >= cc
    nt = (((1,), (1,)), ((), ()))
    tn = (((0,), (0,)), ((), ()))
    gn = gn_ref[...]

    def chunk(c, _):
        start = pl.multiple_of(c * lc, lc)
        cols = colbuf_ref[pl.ds(start, lc), :]
        arow = arow_ref[c]
        for hd in range(nh):
            sl = slice(hd * LANES, (hd + 1) * LANES)
            qk = qk_ref[0, pl.ds(start, lc), sl]
            kr = pltpu.roll(qk, MLSTM_QK_DIM, 1)
            qm = jnp.where(lane < MLSTM_QK_DIM, qk, 0.0).astype(BF16)
            vaug = jnp.concatenate(
                [v_ref[0, pl.ds(start, lc), sl], jnp.broadcast_to(ones_col, (lc, LANES))],
                axis=1)
            sqk = lax.dot_general(qm, kr.astype(BF16), nt, preferred_element_type=F32)
            arg = jnp.where(tril, arow[hd:hd + 1, :] - cols[:, hd:hd + 1], -jnp.inf)
            p = (sqk * jnp.exp(arg)).astype(BF16)
            cst = c_ref[hd]
            nd = (cols[:, 8 + hd:9 + hd]
                  * jnp.dot(qm, cst.astype(BF16), preferred_element_type=F32)
                  + jnp.dot(p, vaug, preferred_element_type=F32))
            den = jnp.maximum(jnp.abs(nd[:, LANES:LANES + 1]), cols[:, 16 + hd:17 + hd])
            hv = nd[:, 0:LANES] / den
            hn = _rms(hv, gn[:, sl])
            og = o_in_ref[0, pl.ds(start, lc), sl].astype(F32)
            out_ref[0, pl.ds(start, lc), sl] = (hn * jax.nn.sigmoid(og)).astype(out_ref.dtype)
            kw = (kr * cols[:, 24 + hd:25 + hd]).astype(BF16)
            c_ref[hd] = (cols[0:1, 32 + hd:33 + hd] * cst
                         + lax.dot_general(kw, vaug, tn, preferred_element_type=F32))
        return 0

    lax.fori_loop(0, nc, chunk, 0)


def _mlstm(gates, bias, mqk, mv, mo, gn):
    b, s, w = mqk.shape
    lc = MLSTM_CHUNK
    seq = lambda bi: (bi, 0, 0)
    return pl.pallas_call(
        _mlstm_kernel,
        grid=(b,),
        in_specs=[
            pl.BlockSpec((1, SUBLANES, s), seq),
            pl.BlockSpec((SUBLANES, 1), lambda bi: (0, 0)),
            pl.BlockSpec((1, s, w), seq),
            pl.BlockSpec((1, s, w), seq),
            pl.BlockSpec((1, s, w), seq),
            pl.BlockSpec((1, w), lambda bi: (0, 0)),
        ],
        out_specs=pl.BlockSpec((1, s, w), seq),
        out_shape=jax.ShapeDtypeStruct((b, s, w), BF16),
        scratch_shapes=[
            pltpu.VMEM((s // lc, SUBLANES, lc), F32),
            pltpu.VMEM((LANES, s), F32),
            pltpu.VMEM((s, LANES), F32),
            pltpu.VMEM((N_MLSTM_HEADS, LANES, 2 * LANES), F32),
        ],
        compiler_params=_params(("arbitrary",)),
        name="mlstm",
    )(gates, bias, mqk, mv, mo, gn)


def _out_mlp_kernel(x_ref, od_ref, om_ref, wod_ref, wom_ref, gpost_ref, gpre_ref,
                    wu_ref, wdn_ref, gmlp_ref, o_ref):
    mixed = (jnp.dot(od_ref[...], wod_ref[...], preferred_element_type=F32)
             + jnp.dot(om_ref[...], wom_ref[...], preferred_element_type=F32))
    x1 = x_ref[...] + _rms(mixed, gpost_ref[...])
    h = _rms(x1, gpre_ref[...]).astype(BF16)
    dff = wu_ref.shape[1]
    acc = None
    for f in range(dff // MLP_FF_CHUNK):
        sl = slice(f * MLP_FF_CHUNK, (f + 1) * MLP_FF_CHUNK)
        a = jnp.maximum(jnp.dot(h, wu_ref[:, sl], preferred_element_type=F32), 0.0)
        part = jnp.dot((a * a).astype(BF16), wdn_ref[sl, :], preferred_element_type=F32)
        acc = part if acc is None else acc + part
    o_ref[...] = x1 + _rms(acc, gmlp_ref[...])


def _out_mlp(x2, od, om, wod, wom, gpost, gpre, wu, wdn, gmlp):
    t, d = x2.shape
    tm = MLP_ROWS
    row = lambda i: (i, 0)
    const = lambda i: (0, 0)
    resident = lambda shape: pl.BlockSpec(shape, const, pipeline_mode=pl.Buffered(1))
    return pl.pallas_call(
        _out_mlp_kernel,
        grid=(t // tm,),
        in_specs=[
            pl.BlockSpec((tm, d), row),
            pl.BlockSpec((tm, od.shape[1]), row),
            pl.BlockSpec((tm, om.shape[1]), row),
            resident(wod.shape),
            resident(wom.shape),
            pl.BlockSpec((1, d), const),
            pl.BlockSpec((1, d), const),
            resident(wu.shape),
            resident(wdn.shape),
            pl.BlockSpec((1, d), const),
        ],
        out_specs=pl.BlockSpec((tm, d), row),
        out_shape=jax.ShapeDtypeStruct((t, d), F32),
        compiler_params=_params(("arbitrary",)),
        name="out_mlp",
    )(x2, od, om, wod, wom, gpost, gpre, wu, wdn, gmlp)


def _rope_tables(seq):
    d = DIFF_HEAD_DIM
    inv = ROPE_THETA ** (-jnp.arange(0, d, 2, dtype=F32) / d)
    ang = jnp.arange(seq, dtype=jnp.int32).astype(F32)[:, None] * inv[None, :]
    cos = jnp.cos(ang)
    sin = jnp.sin(ang)
    zero = jnp.zeros_like(sin)
    reps = LANES // d
    cos_t = jnp.tile(jnp.concatenate([cos, cos], axis=1), (1, reps))
    slo_t = jnp.tile(jnp.concatenate([-sin, zero], axis=1), (1, reps))
    shi_t = jnp.tile(jnp.concatenate([zero, sin], axis=1), (1, reps))
    return cos_t, slo_t, shi_t


def _layer(x, l, norm_mix_pre, w_in, conv_w, conv_b, b_igate, b_fgate, lambda_q1,
           lambda_k1, lambda_q2, lambda_k2, diff_norm, mlstm_norm, w_out,
           norm_mix_post, norm_mlp_pre, w_up, w_down, norm_mlp_post):
    b, s, d = x.shape
    nh, dqk = N_MLSTM_HEADS, MLSTM_QK_DIM
    x2 = x.reshape(b * s, d)

    w = w_in[l]
    wq = w[:, 1536:1792].reshape(d, nh, dqk)
    wk = w[:, 1792:2048].reshape(d, nh, dqk)
    wqk = jnp.concatenate([wq, wk], axis=2).reshape(d, 2 * nh * dqk)
    wg = jnp.pad(w[:, 3072:3080], ((0, 0), (0, LANES - 2 * nh)))
    w_main = jnp.concatenate([w[:, 0:1536], wqk, w[:, 2048:3072], wg], axis=1).astype(BF16)

    def qk_interleave(v):
        lead = v.shape[:-1]
        q = v[..., :nh * dqk].reshape(lead + (nh, dqk))
        k = v[..., nh * dqk:].reshape(lead + (nh, dqk))
        return jnp.concatenate([q, k], axis=-1).reshape(lead + (2 * nh * dqk,))

    cw = qk_interleave(conv_w[l])
    cb = qk_interleave(conv_b[l])[None, :]
    qs = qk_interleave(jnp.concatenate([jnp.full((nh * dqk,), dqk ** -0.5, F32),
                                        jnp.ones((nh * dqk,), F32)]))[None, :]
    cos_t, slo_t, shi_t = _rope_tables(s)

    dq, dk, dv, mqk, mv, mo, gates = _in_proj(
        x2, norm_mix_pre[l][None, :], w_main, cos_t, slo_t, shi_t, cw, cb, qs,
        batch=b, seq=s)

    lam_p = jnp.stack([lambda_q1[l], lambda_k1[l], lambda_q2[l], lambda_k2[l]]).astype(F32)
    o_diff = _diff_attn(lam_p, diff_norm[l][:, None], dq.reshape(b, s, -1),
                        dk.reshape(b, s, -1), dv.reshape(b, s, -1),
                        lam_init=_lambda_init(l))

    bias = jnp.concatenate([b_igate[l], b_fgate[l]]).astype(F32)[:, None]
    o_mlstm = _mlstm(gates, bias, mqk.reshape(b, s, -1), mv.reshape(b, s, -1),
                     mo.reshape(b, s, -1), mlstm_norm[l].reshape(1, -1))

    wo = w_out[l].astype(BF16)
    nd = o_diff.shape[-1]
    out = _out_mlp(x2, o_diff.reshape(b * s, -1), o_mlstm.reshape(b * s, -1),
                   wo[:nd], wo[nd:], norm_mix_post[l][None, :], norm_mlp_pre[l][None, :],
                   w_up[l].astype(BF16), w_down[l].astype(BF16), norm_mlp_post[l][None, :])
    return out.reshape(b, s, d)


def kernel(x, norm_mix_pre, w_in, conv_w, conv_b, b_igate, b_fgate, lambda_q1, lambda_k1,
           lambda_q2, lambda_k2, diff_norm, mlstm_norm, w_out, norm_mix_post,
           norm_mlp_pre, w_up, w_down, norm_mlp_post):
    for l in range(w_in.shape[0]):
        x = _layer(x, l, norm_mix_pre, w_in, conv_w, conv_b, b_igate, b_fgate,
                   lambda_q1, lambda_k1, lambda_q2, lambda_k2, diff_norm, mlstm_norm,
                   w_out, norm_mix_post, norm_mlp_pre, w_up, w_down, norm_mlp_post)
    return x
```

```python
import functools
import math

import numpy as np
import jax
import jax.numpy as jnp
from jax import lax
from jax.experimental import pallas as pl
from jax.experimental.pallas import tpu as pltpu

F32 = jnp.float32
BF16 = jnp.bfloat16

N_DIFF_HEADS = 4
DIFF_HEAD_DIM = 64
N_MLSTM_HEADS = 4
MLSTM_QK_DIM = 64
MLSTM_V_DIM = 128
CONV_WIDTH = 4
ROPE_THETA = 10000.0
EPS = 1e-6

LANES = 128
SUBLANES = 8
VMEM_LIMIT_BYTES = 56 * 1024 * 1024

IN_PROJ_ROWS = 512
LOG2E = math.log2(math.e)
ATTN_ONES_ROWS = 16
ATTN_Q_ROWS = 256
ATTN_KV_ROWS = 256
ATTN_COL_CHUNK = 2048
MLSTM_CHUNK = 256
MLSTM_ONES_ROWS = 16
MLP_ROWS = 512
MLP_FF_CHUNK = 1024


def _lambda_init(layer):
    return 0.8 - 0.6 * math.exp(-0.3 * layer)


def _params(semantics):
    return pltpu.CompilerParams(dimension_semantics=semantics,
                                vmem_limit_bytes=VMEM_LIMIT_BYTES)


def _rms(x, g):
    return x * lax.rsqrt(jnp.mean(x * x, axis=-1, keepdims=True) + EPS) * g


def _in_proj_kernel(x_ref, g_ref, w_ref, cos_ref, slo_ref, shi_ref, cw_ref, cb_ref,
                    qs_ref, dq_ref, dk_ref, dv_ref, qm_ref, kq_ref, mvt_ref, mo_ref,
                    gate_ref, carry_ref, *, tiles_per_seq):
    tm = x_ref.shape[0]
    i = pl.program_id(0)

    @pl.when(i % tiles_per_seq == 0)
    def _():
        carry_ref[...] = jnp.zeros(carry_ref.shape, F32)

    h = _rms(x_ref[...], g_ref[...]).astype(BF16)

    def proj(lo, hi):
        return jnp.dot(h, w_ref[:, lo:hi], preferred_element_type=F32)

    pre = proj(1536, 2048)
    tail = carry_ref[...]
    carry_ref[...] = pre[tm - SUBLANES:tm, :]
    rows = lax.broadcasted_iota(jnp.int32, (SUBLANES, 1), 0)
    cw = cw_ref[...]
    y = cb_ref[...] + cw[3:4, :] * pre
    for j in range(1, CONV_WIDTH):
        sh = pltpu.roll(pre, j, 0)
        head = jnp.where(rows < j, pltpu.roll(tail, j, 0), sh[0:SUBLANES, :])
        sh = jnp.concatenate([head, sh[SUBLANES:, :]], axis=0)
        y = y + cw[3 - j:4 - j, :] * sh
    y = y * jax.nn.sigmoid(y)
    qm_ref[...] = (y * qs_ref[...]).astype(BF16)
    kq_ref[...] = jnp.concatenate(
        [pltpu.roll(y[:, gi * LANES:(gi + 1) * LANES], MLSTM_QK_DIM, 1)
         for gi in range(y.shape[1] // LANES)], axis=1).astype(BF16)

    pmv = proj(2048, 2560)
    lc = MLSTM_CHUNK
    for c in range(tm // lc):
        mvt_ref[0, c] = pmv[c * lc:(c + 1) * lc, :].T.astype(BF16)

    gate_ref[0] = proj(3072, 3200).T[0:SUBLANES, :]

    cos = cos_ref[...]
    s_lo = slo_ref[...]
    s_hi = shi_ref[...]

    def rope(p):
        outs = []
        for gi in range(p.shape[1] // LANES):
            v = p[:, gi * LANES:(gi + 1) * LANES]
            outs.append(v * cos + pltpu.roll(v, LANES - 32, 1) * s_lo
                        + pltpu.roll(v, 32, 1) * s_hi)
        return jnp.concatenate(outs, axis=1)

    dq_ref[...] = (rope(proj(0, 512)) * (DIFF_HEAD_DIM ** -0.5 * LOG2E)).astype(BF16)
    dk_ref[...] = rope(proj(512, 1024)).astype(BF16)
    dv_ref[...] = proj(1024, 1536).astype(BF16)
    mo_ref[...] = proj(2560, 3072).astype(BF16)


def _in_proj(x2, g, w_main, cos_t, slo_t, shi_t, cw, cb, qs, *, batch, seq):
    t, d = x2.shape
    tm = IN_PROJ_ROWS
    lc = MLSTM_CHUNK
    tps = seq // tm
    nw = w_main.shape[1]
    row = lambda i: (i, 0)
    const = lambda i: (0, 0)
    pos = lambda i: (i % tps, 0)
    out_bf = jax.ShapeDtypeStruct((t, 512), BF16)
    return pl.pallas_call(
        functools.partial(_in_proj_kernel, tiles_per_seq=tps),
        grid=(t // tm,),
        in_specs=[
            pl.BlockSpec((tm, d), row),
            pl.BlockSpec((1, d), const),
            pl.BlockSpec((d, nw), const),
            pl.BlockSpec((tm, LANES), pos),
            pl.BlockSpec((tm, LANES), pos),
            pl.BlockSpec((tm, LANES), pos),
            pl.BlockSpec((CONV_WIDTH, 512), const),
            pl.BlockSpec((1, 512), const),
            pl.BlockSpec((1, 512), const),
        ],
        out_specs=[
            pl.BlockSpec((tm, 512), row),
            pl.BlockSpec((tm, 512), row),
            pl.BlockSpec((tm, 512), row),
            pl.BlockSpec((tm, 512), row),
            pl.BlockSpec((tm, 512), row),
            pl.BlockSpec((1, tm // lc, 512, lc), lambda i: (i // tps, i % tps, 0, 0)),
            pl.BlockSpec((tm, 512), row),
            pl.BlockSpec((1, SUBLANES, tm), lambda i: (i // tps, 0, i % tps)),
        ],
        out_shape=[out_bf, out_bf, out_bf, out_bf, out_bf,
                   jax.ShapeDtypeStruct((batch, seq // lc, 512, lc), BF16), out_bf,
                   jax.ShapeDtypeStruct((batch, SUBLANES, seq), F32)],
        scratch_shapes=[pltpu.VMEM((SUBLANES, 512), F32)],
        compiler_params=_params(("arbitrary",)),
        name="in_proj",
    )(x2, g, w_main, cos_t, slo_t, shi_t, cw, cb, qs)


def _diff_attn_kernel(lam_ref, gn_ref, q_ref, k_ref, v_ref, o_ref,
                      acc_ref, m_ref, *, lam_init):
    s = q_ref.shape[1]
    tk = ATTN_KV_ROWS
    tf = ATTN_Q_ROWS
    lp = lam_ref[...]
    lam = (jnp.exp(jnp.sum(lp[0:1] * lp[1:2], axis=-1, keepdims=True))
           - jnp.exp(jnp.sum(lp[2:3] * lp[3:4], axis=-1, keepdims=True)) + lam_init)

    lane = lax.broadcasted_iota(jnp.int32, (1, LANES), 1)
    nt = (((1,), (1,)), ((), ()))
    keep = (lax.broadcasted_iota(jnp.int32, (tk, tk), 1)
            >= lax.broadcasted_iota(jnp.int32, (tk, tk), 0))
    q = q_ref[0]
    zero = jnp.zeros_like(q)
    qmaps = (jnp.where(lane < DIFF_HEAD_DIM, q, zero),
             jnp.where(lane >= DIFF_HEAD_DIM, q, zero))

    m_ref[...] = jnp.full(m_ref.shape, -jnp.inf, F32)
    acc_ref[...] = jnp.zeros(acc_ref.shape, F32)
    ones_rows = jnp.ones((ATTN_ONES_ROWS, tk), BF16)

    cw = ATTN_COL_CHUNK
    units = [(j, mi, c0, min(c0 + cw, s))
             for j in range(s // tk) for c0 in range(j * tk, s, cw) for mi in range(2)]

    def scores(unit):
        j, mi, c0, c1 = unit
        kb = k_ref[0, j * tk:(j + 1) * tk, :]
        st = lax.dot_general(kb, qmaps[mi][c0:c1, :], nt, preferred_element_type=F32)
        if c0 == j * tk:
            diag = jnp.where(keep, st[:, :tk], -jnp.inf)
            st = diag if c1 - c0 == tk else jnp.concatenate([diag, st[:, tk:]], axis=1)
        return st

    vts = {}

    def vt_aug(j):
        if j not in vts:
            lo = j * tk
            vts[j] = jnp.concatenate(
                [jnp.concatenate(
                    [v_ref[0, lo + c * LANES:lo + (c + 1) * LANES, :].astype(F32).T
                     for c in range(tk // LANES)], axis=1).astype(BF16), ones_rows], axis=0)
        return vts[j]

    st_next = scores(units[0])
    for u, (j, mi, c0, c1) in enumerate(units):
        st = st_next
        if u + 1 < len(units):
            st_next = scores(units[u + 1])
        cols = slice(mi * s + c0, mi * s + c1)
        m_old = m_ref[:, cols]
        m_new = jnp.maximum(m_old, jnp.max(st, axis=0, keepdims=True))
        alpha = jnp.exp2(m_old - m_new)
        p = jnp.exp2(st - m_new).astype(BF16)
        m_ref[:, cols] = m_new
        acc_ref[:, cols] = alpha * acc_ref[:, cols] + jnp.dot(
            vt_aug(j), p, preferred_element_type=F32)

    for c0 in range(0, s, tf):
        a1 = acc_ref[0:LANES, c0:c0 + tf] / acc_ref[LANES:LANES + 1, c0:c0 + tf]
        a2 = (acc_ref[0:LANES, s + c0:s + c0 + tf]
              / acc_ref[LANES:LANES + 1, s + c0:s + c0 + tf])
        ot = a1 - lam * a2
        ms = jnp.mean(ot * ot, axis=0, keepdims=True)
        on = ot * lax.rsqrt(ms + EPS) * gn_ref[...] * (1.0 - lam_init)
        o_ref[0, c0:c0 + tf, :] = on.T.astype(o_ref.dtype)


def _diff_attn(lam_p, gn_col, dq, dk, dv, *, lam_init):
    b, s, w = dq.shape
    nh = w // LANES
    head = lambda bi, hi: (bi, 0, hi)
    const = lambda bi, hi: (0, 0)
    return pl.pallas_call(
        functools.partial(_diff_attn_kernel, lam_init=lam_init),
        grid=(b, nh),
        in_specs=[
            pl.BlockSpec((4, DIFF_HEAD_DIM), const),
            pl.BlockSpec((LANES, 1), const),
            pl.BlockSpec((1, s, LANES), head),
            pl.BlockSpec((1, s, LANES), head),
            pl.BlockSpec((1, s, LANES), head),
        ],
        out_specs=pl.BlockSpec((1, s, LANES), head),
        out_shape=jax.ShapeDtypeStruct((b, s, w), BF16),
        scratch_shapes=[
            pltpu.VMEM((LANES + ATTN_ONES_ROWS, 2 * s), F32),
            pltpu.VMEM((1, 2 * s), F32),
        ],
        compiler_params=_params(("arbitrary", "arbitrary")),
        name="diff_attn",
    )(lam_p, gn_col, dq, dk, dv)


def _seg_scan(x, seg_off, seg_len, op, fill, reverse=False):
    n = x.shape[1]
    d = 1
    while d < seg_len:
        if reverse:
            sh = pltpu.roll(x, n - d, 1)
            ok = seg_off < seg_len - d
        else:
            sh = pltpu.roll(x, d, 1)
            ok = seg_off >= d
        x = op(x, jnp.where(ok, sh, fill))
        d *= 2
    return x


def _log_sigmoid(x):
    return -(jnp.maximum(-x, 0.0) + jnp.log1p(jnp.exp(-jnp.abs(x))))


def _mlstm_kernel(gate_ref, bias_ref, qm_ref, kq_ref, vt_ref, o_in_ref, gn_ref, out_ref,
                  rowv_ref, rowbuf_ref, colbuf_ref, ct_ref):
    s = qm_ref.shape[1]
    lc = MLSTM_CHUNK
    nc = s // lc
    nh = N_MLSTM_HEADS

    g = gate_ref[0] + bias_ref[...]
    i_pre = g
    logf = _log_sigmoid(pltpu.roll(g, nh, 0))
    pos = lax.broadcasted_iota(jnp.int32, (1, s), 1)
    seg = pos % lc
    last = seg == lc - 1
    add = lambda a, b_: a + b_
    bcum = _seg_scan(logf, seg, lc, add, 0.0)
    a = i_pre - bcum
    cmax = _seg_scan(a, seg, lc, jnp.maximum, -jnp.inf)
    gb = _seg_scan(jnp.where(last, bcum, 0.0), seg, lc, add, 0.0, reverse=True)
    xb = gb + _seg_scan(jnp.where(last, cmax, -jnp.inf), seg, lc, jnp.maximum,
                        -jnp.inf, reverse=True)
    m_prev = jnp.zeros((SUBLANES, lc), F32)
    mp, mc = [], []
    for c in range(nc):
        m_cur = jnp.maximum(gb[:, c * lc:(c + 1) * lc] + m_prev, xb[:, c * lc:(c + 1) * lc])
        mp.append(m_prev)
        mc.append(m_cur)
        m_prev = m_cur
    mprev = jnp.concatenate(mp, axis=1)
    mcur = jnp.concatenate(mc, axis=1)
    u = jnp.maximum(mprev, cmax)
    rows = (u,
            jnp.exp(mprev - u),
            jnp.exp(-(u + bcum)),
            jnp.exp(gb + a - mcur),
            jnp.exp(gb + mprev - mcur))
    rowbuf_ref[0:SUBLANES, :] = a
    rowbuf_ref[SUBLANES:LANES, :] = jnp.zeros((LANES - SUBLANES, s), F32)
    for c in range(nc):
        for k, rv in enumerate(rows):
            rowv_ref[c, k * SUBLANES:(k + 1) * SUBLANES, :] = rv[:, c * lc:(c + 1) * lc]
        colbuf_ref[c * lc:(c + 1) * lc, :] = rowbuf_ref[:, c * lc:(c + 1) * lc].T

    ct_ref[...] = jnp.zeros(ct_ref.shape, F32)
    ones_rows = jnp.ones((MLSTM_ONES_ROWS, lc), BF16)
    causal = (lax.broadcasted_iota(jnp.int32, (lc, lc), 0)
              <= lax.broadcasted_iota(jnp.int32, (lc, lc), 1))
    nt = (((1,), (1,)), ((), ()))
    gnb = [jnp.broadcast_to(gn_ref[hd * LANES:(hd + 1) * LANES, :], (LANES, lc))
           for hd in range(nh)]

    for c in range(nc):
        start = c * lc
        rv = rowv_ref[c]
        for hd in range(nh):
            sl = slice(hd * LANES, (hd + 1) * LANES)
            qm = qm_ref[0, pl.ds(start, lc), sl]
            kq = kq_ref[0, pl.ds(start, lc), sl]
            vt = jnp.concatenate([vt_ref[0, c, sl, :], ones_rows], axis=0)
            u_row = rv[hd:hd + 1]
            w_inter = rv[8 + hd:9 + hd]
            floor = rv[16 + hd:17 + hd]
            w_row = rv[24 + hd:25 + hd]
            decay = rv[32 + hd:33 + hd, 0:1]
            a_col = colbuf_ref[pl.ds(start, lc), hd:hd + 1]
            st = lax.dot_general(kq, qm, nt, preferred_element_type=F32)
            arg = jnp.where(causal, a_col - u_row, -jnp.inf)
            pt = (st * jnp.exp(arg)).astype(BF16)
            ct = ct_ref[hd]
            nd = (jnp.dot(vt, pt, preferred_element_type=F32)
                  + w_inter * lax.dot_general(ct.astype(BF16), qm, nt,
                                              preferred_element_type=F32))
            den = jnp.maximum(jnp.abs(nd[LANES:LANES + 1]), floor)
            ht = nd[0:LANES] / den
            ms = jnp.mean(ht * ht, axis=0, keepdims=True)
            hn = ht * lax.rsqrt(ms + EPS) * gnb[hd]
            og = o_in_ref[0, pl.ds(start, lc), sl].astype(F32)
            out_ref[0, pl.ds(start, lc), sl] = (hn.T * jax.nn.sigmoid(og)).astype(out_ref.dtype)
            vw = (vt.astype(F32) * w_row).astype(BF16)
            ct_ref[hd] = decay * ct + jnp.dot(vw, kq, preferred_element_type=F32)


def _mlstm(gates, bias, qm, kq, mvt, mo, gn_col):
    b, s, w = qm.shape
    lc = MLSTM_CHUNK
    seq = lambda bi: (bi, 0, 0)
    return pl.pallas_call(
        _mlstm_kernel,
        grid=(b,),
        in_specs=[
            pl.BlockSpec((1, SUBLANES, s), seq),
            pl.BlockSpec((SUBLANES, 1), lambda bi: (0, 0)),
            pl.BlockSpec((1, s, w), seq),
            pl.BlockSpec((1, s, w), seq),
            pl.BlockSpec((1, s // lc, w, lc), lambda bi: (bi, 0, 0, 0)),
            pl.BlockSpec((1, s, w), seq),
            pl.BlockSpec((w, 1), lambda bi: (0, 0)),
        ],
        out_specs=pl.BlockSpec((1, s, w), seq),
        out_shape=jax.ShapeDtypeStruct((b, s, w), BF16),
        scratch_shapes=[
            pltpu.VMEM((s // lc, 5 * SUBLANES, lc), F32),
            pltpu.VMEM((LANES, s), F32),
            pltpu.VMEM((s, LANES), F32),
            pltpu.VMEM((N_MLSTM_HEADS, LANES + MLSTM_ONES_ROWS, LANES), F32),
        ],
        compiler_params=_params(("arbitrary",)),
        name="mlstm",
    )(gates, bias, qm, kq, mvt, mo, gn_col)


def _out_mlp_kernel(x_ref, od_ref, om_ref, wod_ref, wom_ref, gpost_ref, gpre_ref,
                    wu_ref, wdn_ref, gmlp_ref, o_ref):
    mixed = (jnp.dot(od_ref[...], wod_ref[...], preferred_element_type=F32)
             + jnp.dot(om_ref[...], wom_ref[...], preferred_element_type=F32))
    x1 = x_ref[...] + _rms(mixed, gpost_ref[...])
    h = _rms(x1, gpre_ref[...]).astype(BF16)
    dff = wu_ref.shape[1]
    acc = None
    for f in range(dff // MLP_FF_CHUNK):
        sl = slice(f * MLP_FF_CHUNK, (f + 1) * MLP_FF_CHUNK)
        a = jnp.maximum(jnp.dot(h, wu_ref[:, sl], preferred_element_type=F32), 0.0)
        part = jnp.dot((a * a).astype(BF16), wdn_ref[sl, :], preferred_element_type=F32)
        acc = part if acc is None else acc + part
    o_ref[...] = x1 + _rms(acc, gmlp_ref[...])


def _out_mlp(x2, od, om, wod, wom, gpost, gpre, wu, wdn, gmlp):
    t, d = x2.shape
    tm = MLP_ROWS
    row = lambda i: (i, 0)
    const = lambda i: (0, 0)
    resident = lambda shape: pl.BlockSpec(shape, const, pipeline_mode=pl.Buffered(1))
    return pl.pallas_call(
        _out_mlp_kernel,
        grid=(t // tm,),
        in_specs=[
            pl.BlockSpec((tm, d), row),
            pl.BlockSpec((tm, od.shape[1]), row),
            pl.BlockSpec((tm, om.shape[1]), row),
            resident(wod.shape),
            resident(wom.shape),
            pl.BlockSpec((1, d), const),
            pl.BlockSpec((1, d), const),
            resident(wu.shape),
            resident(wdn.shape),
            pl.BlockSpec((1, d), const),
        ],
        out_specs=pl.BlockSpec((tm, d), row),
        out_shape=jax.ShapeDtypeStruct((t, d), F32),
        compiler_params=_params(("arbitrary",)),
        name="out_mlp",
    )(x2, od, om, wod, wom, gpost, gpre, wu, wdn, gmlp)


def _rope_tables(seq):
    d = DIFF_HEAD_DIM
    inv = ROPE_THETA ** (-jnp.arange(0, d, 2, dtype=F32) / d)
    ang = jnp.arange(seq, dtype=jnp.int32).astype(F32)[:, None] * inv[None, :]
    cos = jnp.cos(ang)
    sin = jnp.sin(ang)
    zero = jnp.zeros_like(sin)
    reps = LANES // d
    cos_t = jnp.tile(jnp.concatenate([cos, cos], axis=1), (1, reps))
    slo_t = jnp.tile(jnp.concatenate([-sin, zero], axis=1), (1, reps))
    shi_t = jnp.tile(jnp.concatenate([zero, sin], axis=1), (1, reps))
    return cos_t, slo_t, shi_t


def _layer(x, l, norm_mix_pre, w_in, conv_w, conv_b, b_igate, b_fgate, lambda_q1,
           lambda_k1, lambda_q2, lambda_k2, diff_norm, mlstm_norm, w_out,
           norm_mix_post, norm_mlp_pre, w_up, w_down, norm_mlp_post):
    b, s, d = x.shape
    nh, dqk = N_MLSTM_HEADS, MLSTM_QK_DIM
    x2 = x.reshape(b * s, d)

    w = w_in[l]
    wq = w[:, 1536:1792].reshape(d, nh, dqk)
    wk = w[:, 1792:2048].reshape(d, nh, dqk)
    wqk = jnp.concatenate([wq, wk], axis=2).reshape(d, 2 * nh * dqk)
    wg = jnp.pad(w[:, 3072:3080], ((0, 0), (0, LANES - 2 * nh)))
    w_main = jnp.concatenate([w[:, 0:1536], wqk, w[:, 2048:3072], wg], axis=1).astype(BF16)

    def qk_interleave(v):
        lead = v.shape[:-1]
        q = v[..., :nh * dqk].reshape(lead + (nh, dqk))
        k = v[..., nh * dqk:].reshape(lead + (nh, dqk))
        return jnp.concatenate([q, k], axis=-1).reshape(lead + (2 * nh * dqk,))

    cw = qk_interleave(conv_w[l])
    cb = qk_interleave(conv_b[l])[None, :]
    qs = qk_interleave(jnp.concatenate([jnp.full((nh * dqk,), dqk ** -0.5, F32),
                                        jnp.zeros((nh * dqk,), F32)]))[None, :]
    cos_t, slo_t, shi_t = _rope_tables(s)

    dq, dk, dv, qm, kq, mvt, mo, gates = _in_proj(
        x2, norm_mix_pre[l][None, :], w_main, cos_t, slo_t, shi_t, cw, cb, qs,
        batch=b, seq=s)

    lam_p = jnp.stack([lambda_q1[l], lambda_k1[l], lambda_q2[l], lambda_k2[l]]).astype(F32)
    o_diff = _diff_attn(lam_p, diff_norm[l][:, None], dq.reshape(b, s, -1),
                        dk.reshape(b, s, -1), dv.reshape(b, s, -1),
                        lam_init=_lambda_init(l))

    bias = jnp.concatenate([b_igate[l], b_fgate[l]]).astype(F32)[:, None]
    o_mlstm = _mlstm(gates, bias, qm.reshape(b, s, -1), kq.reshape(b, s, -1), mvt,
                     mo.reshape(b, s, -1), mlstm_norm[l].reshape(-1, 1))

    wo = w_out[l].astype(BF16)
    nd = o_diff.shape[-1]
    out = _out_mlp(x2, o_diff.reshape(b * s, -1), o_mlstm.reshape(b * s, -1),
                   wo[:nd], wo[nd:], norm_mix_post[l][None, :], norm_mlp_pre[l][None, :],
                   w_up[l].astype(BF16), w_down[l].astype(BF16), norm_mlp_post[l][None, :])
    return out.reshape(b, s, d)


def kernel(x, norm_mix_pre, w_in, conv_w, conv_b, b_igate, b_fgate, lambda_q1, lambda_k1,
           lambda_q2, lambda_k2, diff_norm, mlstm_norm, w_out, norm_mix_post,
           norm_mlp_pre, w_up, w_down, norm_mlp_post):
    for l in range(w_in.shape[0]):
        x = _layer(x, l, norm_mix_pre, w_in, conv_w, conv_b, b_igate, b_fgate,
                   lambda_q1, lambda_k1, lambda_q2, lambda_k2, diff_norm, mlstm_norm,
                   w_out, norm_mix_post, norm_mlp_pre, w_up, w_down, norm_mlp_post)
    return x
```

```python
import functools
import math

import numpy as np
import jax
import jax.numpy as jnp
from jax import lax
from jax.experimental import pallas as pl
from jax.experimental.pallas import tpu as pltpu

F32 = jnp.float32
BF16 = jnp.bfloat16

N_DIFF_HEADS = 4
DIFF_HEAD_DIM = 64
N_MLSTM_HEADS = 4
MLSTM_QK_DIM = 64
MLSTM_V_DIM = 128
CONV_WIDTH = 4
ROPE_THETA = 10000.0
EPS = 1e-6

LANES = 128
SUBLANES = 8
VMEM_LIMIT_BYTES = 56 * 1024 * 1024

IN_PROJ_ROWS = 1024
LOG2E = math.log2(math.e)
ATTN_ONES_ROWS = 16
ATTN_Q_ROWS = 256
ATTN_KV_ROWS = 512
ATTN_COL_CHUNK = 2048
MLSTM_CHUNK = 256
MLSTM_ONES_ROWS = 16
MLP_ROWS = 1024
MLP_FF_CHUNK = 1024


def _lambda_init(layer):
    return 0.8 - 0.6 * math.exp(-0.3 * layer)


def _params(semantics, flags=None):
    return pltpu.CompilerParams(dimension_semantics=semantics,
                                vmem_limit_bytes=VMEM_LIMIT_BYTES, flags=flags)


def _rms(x, g):
    return x * lax.rsqrt(jnp.mean(x * x, axis=-1, keepdims=True) + EPS) * g


def _in_proj_kernel(x_ref, g_ref, w_ref, cos_ref, slo_ref, shi_ref, cw_ref, cb_ref,
                    qs_ref, dq_ref, dk_ref, dv_ref, qm_ref, kq_ref, mvt_ref, mo_ref,
                    gate_ref, carry_ref, *, tiles_per_seq):
    tm = x_ref.shape[0]
    i = pl.program_id(0)

    @pl.when(i % tiles_per_seq == 0)
    def _():
        carry_ref[...] = jnp.zeros(carry_ref.shape, F32)

    h = _rms(x_ref[...], g_ref[...]).astype(BF16)

    def proj(lo, hi):
        return jnp.dot(h, w_ref[:, lo:hi], preferred_element_type=F32)

    pre = proj(1536, 2048)
    tail = carry_ref[...]
    carry_ref[...] = pre[tm - SUBLANES:tm, :]
    rows = lax.broadcasted_iota(jnp.int32, (SUBLANES, 1), 0)
    cw = cw_ref[...]
    y = cb_ref[...] + cw[3:4, :] * pre
    for j in range(1, CONV_WIDTH):
        sh = pltpu.roll(pre, j, 0)
        head = jnp.where(rows < j, pltpu.roll(tail, j, 0), sh[0:SUBLANES, :])
        sh = jnp.concatenate([head, sh[SUBLANES:, :]], axis=0)
        y = y + cw[3 - j:4 - j, :] * sh
    y = y * jax.nn.sigmoid(y)
    qm_ref[...] = (y * qs_ref[...]).astype(BF16)
    kq_ref[...] = jnp.concatenate(
        [pltpu.roll(y[:, gi * LANES:(gi + 1) * LANES], MLSTM_QK_DIM, 1)
         for gi in range(y.shape[1] // LANES)], axis=1).astype(BF16)

    pmv = proj(2048, 2560)
    lc = MLSTM_CHUNK
    for c in range(tm // lc):
        mvt_ref[0, c] = pmv[c * lc:(c + 1) * lc, :].T.astype(BF16)

    gate_ref[0] = proj(3072, 3200).T[0:SUBLANES, :]

    cos = cos_ref[...]
    s_lo = slo_ref[...]
    s_hi = shi_ref[...]

    def rope(p):
        outs = []
        for gi in range(p.shape[1] // LANES):
            v = p[:, gi * LANES:(gi + 1) * LANES]
            outs.append(v * cos + pltpu.roll(v, LANES - 32, 1) * s_lo
                        + pltpu.roll(v, 32, 1) * s_hi)
        return jnp.concatenate(outs, axis=1)

    dq_ref[...] = (rope(proj(0, 512)) * (DIFF_HEAD_DIM ** -0.5 * LOG2E)).astype(BF16)
    dk_ref[...] = rope(proj(512, 1024)).astype(BF16)
    dv_ref[...] = proj(1024, 1536).astype(BF16)
    mo_ref[...] = proj(2560, 3072).astype(BF16)


def _in_proj(x2, g, w_main, cos_t, slo_t, shi_t, cw, cb, qs, *, batch, seq):
    t, d = x2.shape
    tm = IN_PROJ_ROWS
    lc = MLSTM_CHUNK
    tps = seq // tm
    nw = w_main.shape[1]
    row = lambda i: (i, 0)
    const = lambda i: (0, 0)
    pos = lambda i: (i % tps, 0)
    out_bf = jax.ShapeDtypeStruct((t, 512), BF16)
    return pl.pallas_call(
        functools.partial(_in_proj_kernel, tiles_per_seq=tps),
        grid=(t // tm,),
        in_specs=[
            pl.BlockSpec((tm, d), row),
            pl.BlockSpec((1, d), const),
            pl.BlockSpec((d, nw), const, pipeline_mode=pl.Buffered(1)),
            pl.BlockSpec((tm, LANES), pos),
            pl.BlockSpec((tm, LANES), pos),
            pl.BlockSpec((tm, LANES), pos),
            pl.BlockSpec((CONV_WIDTH, 512), const),
            pl.BlockSpec((1, 512), const),
            pl.BlockSpec((1, 512), const),
        ],
        out_specs=[
            pl.BlockSpec((tm, 512), row),
            pl.BlockSpec((tm, 512), row),
            pl.BlockSpec((tm, 512), row),
            pl.BlockSpec((tm, 512), row),
            pl.BlockSpec((tm, 512), row),
            pl.BlockSpec((1, tm // lc, 512, lc), lambda i: (i // tps, i % tps, 0, 0)),
            pl.BlockSpec((tm, 512), row),
            pl.BlockSpec((1, SUBLANES, tm), lambda i: (i // tps, 0, i % tps)),
        ],
        out_shape=[out_bf, out_bf, out_bf, out_bf, out_bf,
                   jax.ShapeDtypeStruct((batch, seq // lc, 512, lc), BF16), out_bf,
                   jax.ShapeDtypeStruct((batch, SUBLANES, seq), F32)],
        scratch_shapes=[pltpu.VMEM((SUBLANES, 512), F32)],
        compiler_params=_params(("arbitrary",)),
        name="in_proj",
    )(x2, g, w_main, cos_t, slo_t, shi_t, cw, cb, qs)


def _diff_attn_kernel(lam_ref, gn_ref, q_ref, k_ref, v_ref, o_ref,
                      acc_ref, m_ref, *, lam_init):
    s = q_ref.shape[1]
    tk = ATTN_KV_ROWS
    tf = ATTN_Q_ROWS
    lp = lam_ref[...]
    lam = (jnp.exp(jnp.sum(lp[0:1] * lp[1:2], axis=-1, keepdims=True))
           - jnp.exp(jnp.sum(lp[2:3] * lp[3:4], axis=-1, keepdims=True)) + lam_init)

    lane = lax.broadcasted_iota(jnp.int32, (1, LANES), 1)
    nt = (((1,), (1,)), ((), ()))
    keep = (lax.broadcasted_iota(jnp.int32, (tk, tk), 1)
            >= lax.broadcasted_iota(jnp.int32, (tk, tk), 0))
    q = q_ref[0]
    zero = jnp.zeros_like(q)
    qmaps = (jnp.where(lane < DIFF_HEAD_DIM, q, zero),
             jnp.where(lane >= DIFF_HEAD_DIM, q, zero))

    ones_rows = jnp.ones((ATTN_ONES_ROWS, tk), BF16)

    cw = ATTN_COL_CHUNK
    units = [(j, mi, c0, min(c0 + cw, s))
             for j in range(s // tk) for c0 in range(j * tk, s, cw) for mi in range(2)]

    def scores(unit):
        j, mi, c0, c1 = unit
        kb = k_ref[0, j * tk:(j + 1) * tk, :]
        st = lax.dot_general(kb, qmaps[mi][c0:c1, :], nt, preferred_element_type=F32)
        if c0 == j * tk:
            diag = jnp.where(keep, st[:, :tk], -jnp.inf)
            st = diag if c1 - c0 == tk else jnp.concatenate([diag, st[:, tk:]], axis=1)
        return st

    vts = {}

    def vt_aug(j):
        if j not in vts:
            lo = j * tk
            vts[j] = jnp.concatenate(
                [jnp.concatenate(
                    [v_ref[0, lo + c * LANES:lo + (c + 1) * LANES, :].astype(F32).T
                     for c in range(tk // LANES)], axis=1).astype(BF16), ones_rows], axis=0)
        return vts[j]

    st_next = scores(units[0])
    for u, (j, mi, c0, c1) in enumerate(units):
        st = st_next
        if u + 1 < len(units):
            st_next = scores(units[u + 1])
        cols = slice(mi * s + c0, mi * s + c1)
        m_blk = jnp.max(st, axis=0, keepdims=True)
        if j == 0:
            m_ref[:, cols] = m_blk
            acc_ref[:, cols] = jnp.dot(vt_aug(j), jnp.exp2(st - m_blk).astype(BF16),
                                       preferred_element_type=F32)
            continue
        m_old = m_ref[:, cols]
        m_new = jnp.maximum(m_old, m_blk)
        alpha = jnp.exp2(m_old - m_new)
        p = jnp.exp2(st - m_new).astype(BF16)
        m_ref[:, cols] = m_new
        acc_ref[:, cols] = alpha * acc_ref[:, cols] + jnp.dot(
            vt_aug(j), p, preferred_element_type=F32)

    for c0 in range(0, s, tf):
        a1 = acc_ref[0:LANES, c0:c0 + tf] / acc_ref[LANES:LANES + 1, c0:c0 + tf]
        a2 = (acc_ref[0:LANES, s + c0:s + c0 + tf]
              / acc_ref[LANES:LANES + 1, s + c0:s + c0 + tf])
        ot = a1 - lam * a2
        ms = jnp.mean(ot * ot, axis=0, keepdims=True)
        on = ot * lax.rsqrt(ms + EPS) * gn_ref[...] * (1.0 - lam_init)
        o_ref[0, c0:c0 + tf, :] = on.T.astype(o_ref.dtype)


def _diff_attn(lam_p, gn_col, dq, dk, dv, *, lam_init):
    b, s, w = dq.shape
    nh = w // LANES
    head = lambda bi, hi: (bi, 0, hi)
    const = lambda bi, hi: (0, 0)
    return pl.pallas_call(
        functools.partial(_diff_attn_kernel, lam_init=lam_init),
        grid=(b, nh),
        in_specs=[
            pl.BlockSpec((4, DIFF_HEAD_DIM), const),
            pl.BlockSpec((LANES, 1), const),
            pl.BlockSpec((1, s, LANES), head),
            pl.BlockSpec((1, s, LANES), head),
            pl.BlockSpec((1, s, LANES), head),
        ],
        out_specs=pl.BlockSpec((1, s, LANES), head),
        out_shape=jax.ShapeDtypeStruct((b, s, w), BF16),
        scratch_shapes=[
            pltpu.VMEM((LANES + ATTN_ONES_ROWS, 2 * s), F32),
            pltpu.VMEM((1, 2 * s), F32),
        ],
        compiler_params=_params(("arbitrary", "arbitrary")),
        name="diff_attn",
    )(lam_p, gn_col, dq, dk, dv)


def _seg_scan(x, seg_off, seg_len, op, fill, reverse=False):
    n = x.shape[1]
    d = 1
    while d < seg_len:
        if reverse:
            sh = pltpu.roll(x, n - d, 1)
            ok = seg_off < seg_len - d
        else:
            sh = pltpu.roll(x, d, 1)
            ok = seg_off >= d
        x = op(x, jnp.where(ok, sh, fill))
        d *= 2
    return x


def _log_sigmoid(x):
    return -(jnp.maximum(-x, 0.0) + jnp.log1p(jnp.exp(-jnp.abs(x))))


def _mlstm_kernel(gate_ref, bias_ref, qm_ref, kq_ref, vt_ref, o_in_ref, gn_ref, out_ref,
                  rowv_ref, rowbuf_ref, colbuf_ref, ct_ref):
    s = qm_ref.shape[1]
    lc = MLSTM_CHUNK
    nc = s // lc
    nh = N_MLSTM_HEADS

    g = gate_ref[0] + bias_ref[...]
    i_pre = g
    logf = _log_sigmoid(pltpu.roll(g, nh, 0))
    pos = lax.broadcasted_iota(jnp.int32, (1, s), 1)
    seg = pos % lc
    last = seg == lc - 1
    add = lambda a, b_: a + b_
    bcum = _seg_scan(logf, seg, lc, add, 0.0)
    a = i_pre - bcum
    cmax = _seg_scan(a, seg, lc, jnp.maximum, -jnp.inf)
    gb = _seg_scan(jnp.where(last, bcum, 0.0), seg, lc, add, 0.0, reverse=True)
    xb = gb + _seg_scan(jnp.where(last, cmax, -jnp.inf), seg, lc, jnp.maximum,
                        -jnp.inf, reverse=True)
    m_prev = jnp.zeros((SUBLANES, lc), F32)
    mp, mc = [], []
    for c in range(nc):
        m_cur = jnp.maximum(gb[:, c * lc:(c + 1) * lc] + m_prev, xb[:, c * lc:(c + 1) * lc])
        mp.append(m_prev)
        mc.append(m_cur)
        m_prev = m_cur
    mprev = jnp.concatenate(mp, axis=1)
    mcur = jnp.concatenate(mc, axis=1)
    u = jnp.maximum(mprev, cmax)
    rows = (u,
            jnp.exp(mprev - u),
            jnp.exp(-(u + bcum)),
            jnp.exp(gb + a - mcur),
            jnp.exp(gb + mprev - mcur))
    rowbuf_ref[0:SUBLANES, :] = a
    rowbuf_ref[SUBLANES:LANES, :] = jnp.zeros((LANES - SUBLANES, s), F32)
    for c in range(nc):
        for k, rv in enumerate(rows):
            rowv_ref[c, k * SUBLANES:(k + 1) * SUBLANES, :] = rv[:, c * lc:(c + 1) * lc]
        colbuf_ref[c * lc:(c + 1) * lc, :] = rowbuf_ref[:, c * lc:(c + 1) * lc].T

    ct_ref[...] = jnp.zeros(ct_ref.shape, F32)
    ones_rows = jnp.ones((MLSTM_ONES_ROWS, lc), BF16)
    causal = (lax.broadcasted_iota(jnp.int32, (lc, lc), 0)
              <= lax.broadcasted_iota(jnp.int32, (lc, lc), 1))
    nt = (((1,), (1,)), ((), ()))
    gnb = [jnp.broadcast_to(gn_ref[hd * LANES:(hd + 1) * LANES, :], (LANES, lc))
           for hd in range(nh)]

    for c in range(nc):
        start = c * lc
        rv = rowv_ref[c]
        for hd in range(nh):
            sl = slice(hd * LANES, (hd + 1) * LANES)
            qm = qm_ref[0, pl.ds(start, lc), sl]
            kq = kq_ref[0, pl.ds(start, lc), sl]
            vt = jnp.concatenate([vt_ref[0, c, sl, :], ones_rows], axis=0)
            u_row = rv[hd:hd + 1]
            w_inter = rv[8 + hd:9 + hd]
            floor = rv[16 + hd:17 + hd]
            w_row = rv[24 + hd:25 + hd]
            decay = rv[32 + hd:33 + hd, 0:1]
            a_col = colbuf_ref[pl.ds(start, lc), hd:hd + 1]
            st = lax.dot_general(kq, qm, nt, preferred_element_type=F32)
            arg = jnp.where(causal, a_col - u_row, -jnp.inf)
            pt = (st * jnp.exp(arg)).astype(BF16)
            ct = ct_ref[hd]
            nd = (jnp.dot(vt, pt, preferred_element_type=F32)
                  + w_inter * lax.dot_general(ct.astype(BF16), qm, nt,
                                              preferred_element_type=F32))
            den = jnp.maximum(jnp.abs(nd[LANES:LANES + 1]), floor)
            ht = nd[0:LANES] / den
            ms = jnp.mean(ht * ht, axis=0, keepdims=True)
            hn = ht * lax.rsqrt(ms + EPS) * gnb[hd]
            og = o_in_ref[0, pl.ds(start, lc), sl].astype(F32)
            out_ref[0, pl.ds(start, lc), sl] = (hn.T * jax.nn.sigmoid(og)).astype(out_ref.dtype)
            vw = (vt.astype(F32) * w_row).astype(BF16)
            ct_ref[hd] = decay * ct + jnp.dot(vw, kq, preferred_element_type=F32)


def _mlstm(gates, bias, qm, kq, mvt, mo, gn_col):
    b, s, w = qm.shape
    lc = MLSTM_CHUNK
    seq = lambda bi: (bi, 0, 0)
    return pl.pallas_call(
        _mlstm_kernel,
        grid=(b,),
        in_specs=[
            pl.BlockSpec((1, SUBLANES, s), seq),
            pl.BlockSpec((SUBLANES, 1), lambda bi: (0, 0)),
            pl.BlockSpec((1, s, w), seq),
            pl.BlockSpec((1, s, w), seq),
            pl.BlockSpec((1, s // lc, w, lc), lambda bi: (bi, 0, 0, 0)),
            pl.BlockSpec((1, s, w), seq),
            pl.BlockSpec((w, 1), lambda bi: (0, 0)),
        ],
        out_specs=pl.BlockSpec((1, s, w), seq),
        out_shape=jax.ShapeDtypeStruct((b, s, w), BF16),
        scratch_shapes=[
            pltpu.VMEM((s // lc, 5 * SUBLANES, lc), F32),
            pltpu.VMEM((LANES, s), F32),
            pltpu.VMEM((s, LANES), F32),
            pltpu.VMEM((N_MLSTM_HEADS, LANES + MLSTM_ONES_ROWS, LANES), F32),
        ],
        compiler_params=_params(("arbitrary",)),
        name="mlstm",
    )(gates, bias, qm, kq, mvt, mo, gn_col)


def _out_mlp_kernel(x_ref, od_ref, om_ref, wod_ref, wom_ref, gpost_ref, gpre_ref,
                    wu_ref, wdn_ref, gmlp_ref, o_ref):
    mixed = (jnp.dot(od_ref[...], wod_ref[...], preferred_element_type=F32)
             + jnp.dot(om_ref[...], wom_ref[...], preferred_element_type=F32))
    x1 = x_ref[...] + _rms(mixed, gpost_ref[...])
    h = _rms(x1, gpre_ref[...]).astype(BF16)
    dff = wu_ref.shape[1]
    acc = None
    for f in range(dff // MLP_FF_CHUNK):
        sl = slice(f * MLP_FF_CHUNK, (f + 1) * MLP_FF_CHUNK)
        a = jnp.maximum(jnp.dot(h, wu_ref[:, sl], preferred_element_type=F32), 0.0)
        part = jnp.dot((a * a).astype(BF16), wdn_ref[sl, :], preferred_element_type=F32)
        acc = part if acc is None else acc + part
    o_ref[...] = x1 + _rms(acc, gmlp_ref[...])


def _out_mlp(x2, od, om, wod, wom, gpost, gpre, wu, wdn, gmlp):
    t, d = x2.shape
    tm = MLP_ROWS
    row = lambda i: (i, 0)
    const = lambda i: (0, 0)
    resident = lambda shape: pl.BlockSpec(shape, const, pipeline_mode=pl.Buffered(1))
    return pl.pallas_call(
        _out_mlp_kernel,
        grid=(t // tm,),
        in_specs=[
            pl.BlockSpec((tm, d), row),
            pl.BlockSpec((tm, od.shape[1]), row),
            pl.BlockSpec((tm, om.shape[1]), row),
            resident(wod.shape),
            resident(wom.shape),
            pl.BlockSpec((1, d), const),
            pl.BlockSpec((1, d), const),
            resident(wu.shape),
            resident(wdn.shape),
            pl.BlockSpec((1, d), const),
        ],
        out_specs=pl.BlockSpec((tm, d), row),
        out_shape=jax.ShapeDtypeStruct((t, d), F32),
        compiler_params=_params(("arbitrary",)),
        name="out_mlp",
    )(x2, od, om, wod, wom, gpost, gpre, wu, wdn, gmlp)


def _rope_tables(seq):
    d = DIFF_HEAD_DIM
    inv = ROPE_THETA ** (-jnp.arange(0, d, 2, dtype=F32) / d)
    ang = jnp.arange(seq, dtype=jnp.int32).astype(F32)[:, None] * inv[None, :]
    cos = jnp.cos(ang)
    sin = jnp.sin(ang)
    zero = jnp.zeros_like(sin)
    reps = LANES // d
    cos_t = jnp.tile(jnp.concatenate([cos, cos], axis=1), (1, reps))
    slo_t = jnp.tile(jnp.concatenate([-sin, zero], axis=1), (1, reps))
    shi_t = jnp.tile(jnp.concatenate([zero, sin], axis=1), (1, reps))
    return cos_t, slo_t, shi_t


def _layer(x, l, norm_mix_pre, w_in, conv_w, conv_b, b_igate, b_fgate, lambda_q1,
           lambda_k1, lambda_q2, lambda_k2, diff_norm, mlstm_norm, w_out,
           norm_mix_post, norm_mlp_pre, w_up, w_down, norm_mlp_post):
    b, s, d = x.shape
    nh, dqk = N_MLSTM_HEADS, MLSTM_QK_DIM
    x2 = x.reshape(b * s, d)

    w = w_in[l]
    wq = w[:, 1536:1792].reshape(d, nh, dqk)
    wk = w[:, 1792:2048].reshape(d, nh, dqk)
    wqk = jnp.concatenate([wq, wk], axis=2).reshape(d, 2 * nh * dqk)
    wg = jnp.pad(w[:, 3072:3080], ((0, 0), (0, LANES - 2 * nh)))
    w_main = jnp.concatenate([w[:, 0:1536], wqk, w[:, 2048:3072], wg], axis=1).astype(BF16)

    def qk_interleave(v):
        lead = v.shape[:-1]
        q = v[..., :nh * dqk].reshape(lead + (nh, dqk))
        k = v[..., nh * dqk:].reshape(lead + (nh, dqk))
        return jnp.concatenate([q, k], axis=-1).reshape(lead + (2 * nh * dqk,))

    cw = qk_interleave(conv_w[l])
    cb = qk_interleave(conv_b[l])[None, :]
    qs = qk_interleave(jnp.concatenate([jnp.full((nh * dqk,), dqk ** -0.5, F32),
                                        jnp.zeros((nh * dqk,), F32)]))[None, :]
    cos_t, slo_t, shi_t = _rope_tables(s)

    dq, dk, dv, qm, kq, mvt, mo, gates = _in_proj(
        x2, norm_mix_pre[l][None, :], w_main, cos_t, slo_t, shi_t, cw, cb, qs,
        batch=b, seq=s)

    lam_p = jnp.stack([lambda_q1[l], lambda_k1[l], lambda_q2[l], lambda_k2[l]]).astype(F32)
    o_diff = _diff_attn(lam_p, diff_norm[l][:, None], dq.reshape(b, s, -1),
                        dk.reshape(b, s, -1), dv.reshape(b, s, -1),
                        lam_init=_lambda_init(l))

    bias = jnp.concatenate([b_igate[l], b_fgate[l]]).astype(F32)[:, None]
    o_mlstm = _mlstm(gates, bias, qm.reshape(b, s, -1), kq.reshape(b, s, -1), mvt,
                     mo.reshape(b, s, -1), mlstm_norm[l].reshape(-1, 1))

    wo = w_out[l].astype(BF16)
    nd = o_diff.shape[-1]
    out = _out_mlp(x2, o_diff.reshape(b * s, -1), o_mlstm.reshape(b * s, -1),
                   wo[:nd], wo[nd:], norm_mix_post[l][None, :], norm_mlp_pre[l][None, :],
                   w_up[l].astype(BF16), w_down[l].astype(BF16), norm_mlp_post[l][None, :])
    return out.reshape(b, s, d)


def kernel(x, norm_mix_pre, w_in, conv_w, conv_b, b_igate, b_fgate, lambda_q1, lambda_k1,
           lambda_q2, lambda_k2, diff_norm, mlstm_norm, w_out, norm_mix_post,
           norm_mlp_pre, w_up, w_down, norm_mlp_post):
    for l in range(w_in.shape[0]):
        x = _layer(x, l, norm_mix_pre, w_in, conv_w, conv_b, b_igate, b_fgate,
                   lambda_q1, lambda_k1, lambda_q2, lambda_k2, diff_norm, mlstm_norm,
                   w_out, norm_mix_post, norm_mlp_pre, w_up, w_down, norm_mlp_post)
    return x
```

```python
import functools
import math

import numpy as np
import jax
import jax.numpy as jnp
from jax import lax
from jax.experimental import pallas as pl
from jax.experimental.pallas import tpu as pltpu

F32 = jnp.float32
BF16 = jnp.bfloat16

N_DIFF_HEADS = 4
DIFF_HEAD_DIM = 64
N_MLSTM_HEADS = 4
MLSTM_QK_DIM = 64
MLSTM_V_DIM = 128
CONV_WIDTH = 4
ROPE_THETA = 10000.0
EPS = 1e-6

LANES = 128
SUBLANES = 8
VMEM_LIMIT_BYTES = 56 * 1024 * 1024

IN_PROJ_ROWS = 1024
LOG2E = math.log2(math.e)
ATTN_ONES_ROWS = 16
ATTN_Q_ROWS = 256
ATTN_KV_ROWS = 512
ATTN_COL_CHUNK = 2048
MLSTM_CHUNK = 256
MLSTM_ONES_ROWS = 16
MLSTM_GATE_SEQS = 8
MLP_ROWS = 1024
MLP_FF_CHUNK = 1024


def _lambda_init(layer):
    return 0.8 - 0.6 * math.exp(-0.3 * layer)


def _params(semantics, flags=None):
    return pltpu.CompilerParams(dimension_semantics=semantics,
                                vmem_limit_bytes=VMEM_LIMIT_BYTES, flags=flags)


def _rms(x, g):
    return x * lax.rsqrt(jnp.mean(x * x, axis=-1, keepdims=True) + EPS) * g


def _in_proj_kernel(x_ref, g_ref, w_ref, cos_ref, slo_ref, shi_ref, cw_ref, cb_ref,
                    qs_ref, dq_ref, dk_ref, dv_ref, qm_ref, kq_ref, mvt_ref, og_ref,
                    gate_ref, carry_ref, *, tiles_per_seq):
    tm = x_ref.shape[0]
    i = pl.program_id(0)

    @pl.when(i % tiles_per_seq == 0)
    def _():
        carry_ref[...] = jnp.zeros(carry_ref.shape, F32)

    h = _rms(x_ref[...], g_ref[...]).astype(BF16)

    def proj(lo, hi):
        return jnp.dot(h, w_ref[:, lo:hi], preferred_element_type=F32)

    pre = proj(1536, 2048)
    tail = carry_ref[...]
    carry_ref[...] = pre[tm - SUBLANES:tm, :]
    rows = lax.broadcasted_iota(jnp.int32, (SUBLANES, 1), 0)
    cw = cw_ref[...]
    y = cb_ref[...] + cw[3:4, :] * pre
    for j in range(1, CONV_WIDTH):
        sh = pltpu.roll(pre, j, 0)
        head = jnp.where(rows < j, pltpu.roll(tail, j, 0), sh[0:SUBLANES, :])
        sh = jnp.concatenate([head, sh[SUBLANES:, :]], axis=0)
        y = y + cw[3 - j:4 - j, :] * sh
    y = y * jax.nn.sigmoid(y)
    qm_ref[...] = (y * qs_ref[...]).astype(BF16)
    kq_ref[...] = jnp.concatenate(
        [pltpu.roll(y[:, gi * LANES:(gi + 1) * LANES], MLSTM_QK_DIM, 1)
         for gi in range(y.shape[1] // LANES)], axis=1).astype(BF16)

    pmv = proj(2048, 2560)
    lc = MLSTM_CHUNK
    for c in range(tm // lc):
        mvt_ref[0, c] = pmv[c * lc:(c + 1) * lc, :].T.astype(BF16)

    gate_ref[0] = proj(3072, 3200).T[0:SUBLANES, :]
    og_ref[...] = jax.nn.sigmoid(proj(2560, 3072)).astype(BF16)

    cos = cos_ref[...]
    s_lo = slo_ref[...]
    s_hi = shi_ref[...]

    def rope(p):
        outs = []
        for gi in range(p.shape[1] // LANES):
            v = p[:, gi * LANES:(gi + 1) * LANES]
            outs.append(v * cos + pltpu.roll(v, LANES - 32, 1) * s_lo
                        + pltpu.roll(v, 32, 1) * s_hi)
        return jnp.concatenate(outs, axis=1)

    dq_ref[...] = (rope(proj(0, 512)) * (DIFF_HEAD_DIM ** -0.5 * LOG2E)).astype(BF16)
    dk_ref[...] = rope(proj(512, 1024)).astype(BF16)
    dv_ref[...] = proj(1024, 1536).astype(BF16)


def _in_proj(x2, g, w_main, cos_t, slo_t, shi_t, cw, cb, qs, *, batch, seq):
    t, d = x2.shape
    tm = IN_PROJ_ROWS
    lc = MLSTM_CHUNK
    tps = seq // tm
    nw = w_main.shape[1]
    row = lambda i: (i, 0)
    const = lambda i: (0, 0)
    pos = lambda i: (i % tps, 0)
    out_bf = jax.ShapeDtypeStruct((t, 512), BF16)
    return pl.pallas_call(
        functools.partial(_in_proj_kernel, tiles_per_seq=tps),
        grid=(t // tm,),
        in_specs=[
            pl.BlockSpec((tm, d), row),
            pl.BlockSpec((1, d), const),
            pl.BlockSpec((d, nw), const, pipeline_mode=pl.Buffered(1)),
            pl.BlockSpec((tm, LANES), pos),
            pl.BlockSpec((tm, LANES), pos),
            pl.BlockSpec((tm, LANES), pos),
            pl.BlockSpec((CONV_WIDTH, 512), const),
            pl.BlockSpec((1, 512), const),
            pl.BlockSpec((1, 512), const),
        ],
        out_specs=[
            pl.BlockSpec((tm, 512), row),
            pl.BlockSpec((tm, 512), row),
            pl.BlockSpec((tm, 512), row),
            pl.BlockSpec((tm, 512), row),
            pl.BlockSpec((tm, 512), row),
            pl.BlockSpec((1, tm // lc, 512, lc), lambda i: (i // tps, i % tps, 0, 0)),
            pl.BlockSpec((tm, 512), row),
            pl.BlockSpec((1, SUBLANES, tm), lambda i: (i // tps, 0, i % tps)),
        ],
        out_shape=[out_bf, out_bf, out_bf, out_bf, out_bf,
                   jax.ShapeDtypeStruct((batch, seq // lc, 512, lc), BF16), out_bf,
                   jax.ShapeDtypeStruct((batch, SUBLANES, seq), F32)],
        scratch_shapes=[pltpu.VMEM((SUBLANES, 512), F32)],
        compiler_params=_params(("arbitrary",)),
        name="in_proj",
    )(x2, g, w_main, cos_t, slo_t, shi_t, cw, cb, qs)


def _diff_attn_kernel(lam_ref, gn_ref, q_ref, k_ref, v_ref, o_ref,
                      acc_ref, m_ref, *, lam_init):
    s = q_ref.shape[1]
    tk = ATTN_KV_ROWS
    tf = ATTN_Q_ROWS
    lp = lam_ref[...]
    lam = (jnp.exp(jnp.sum(lp[0:1] * lp[1:2], axis=-1, keepdims=True))
           - jnp.exp(jnp.sum(lp[2:3] * lp[3:4], axis=-1, keepdims=True)) + lam_init)

    lane = lax.broadcasted_iota(jnp.int32, (1, LANES), 1)
    nt = (((1,), (1,)), ((), ()))
    keep = (lax.broadcasted_iota(jnp.int32, (tk, tk), 1)
            >= lax.broadcasted_iota(jnp.int32, (tk, tk), 0))
    q = q_ref[0]
    zero = jnp.zeros_like(q)
    qmaps = (jnp.where(lane < DIFF_HEAD_DIM, q, zero),
             jnp.where(lane >= DIFF_HEAD_DIM, q, zero))

    ones_rows = jnp.ones((ATTN_ONES_ROWS, tk), BF16)

    cw = ATTN_COL_CHUNK
    units = [(j, mi, c0, min(c0 + cw, s))
             for j in range(s // tk) for c0 in range(j * tk, s, cw) for mi in range(2)]

    def scores(unit):
        j, mi, c0, c1 = unit
        kb = k_ref[0, j * tk:(j + 1) * tk, :]
        st = lax.dot_general(kb, qmaps[mi][c0:c1, :], nt, preferred_element_type=F32)
        if c0 == j * tk:
            diag = jnp.where(keep, st[:, :tk], -jnp.inf)
            st = diag if c1 - c0 == tk else jnp.concatenate([diag, st[:, tk:]], axis=1)
        return st

    vts = {}

    def vt_aug(j):
        if j not in vts:
            lo = j * tk
            vts[j] = jnp.concatenate(
                [jnp.concatenate(
                    [v_ref[0, lo + c * LANES:lo + (c + 1) * LANES, :].astype(F32).T
                     for c in range(tk // LANES)], axis=1).astype(BF16), ones_rows], axis=0)
        return vts[j]

    st_next = scores(units[0])
    for u, (j, mi, c0, c1) in enumerate(units):
        st = st_next
        if u + 1 < len(units):
            st_next = scores(units[u + 1])
        cols = slice(mi * s + c0, mi * s + c1)
        m_blk = jnp.max(st, axis=0, keepdims=True)
        if j == 0:
            m_ref[:, cols] = m_blk
            acc_ref[:, cols] = jnp.dot(vt_aug(j), jnp.exp2(st - m_blk).astype(BF16),
                                       preferred_element_type=F32)
            continue
        m_old = m_ref[:, cols]
        m_new = jnp.maximum(m_old, m_blk)
        alpha = jnp.exp2(m_old - m_new)
        p = jnp.exp2(st - m_new).astype(BF16)
        m_ref[:, cols] = m_new
        acc_ref[:, cols] = alpha * acc_ref[:, cols] + jnp.dot(
            vt_aug(j), p, preferred_element_type=F32)

    for c0 in range(0, s, tf):
        a1 = acc_ref[0:LANES, c0:c0 + tf] / acc_ref[LANES:LANES + 1, c0:c0 + tf]
        a2 = (acc_ref[0:LANES, s + c0:s + c0 + tf]
              / acc_ref[LANES:LANES + 1, s + c0:s + c0 + tf])
        ot = a1 - lam * a2
        ms = jnp.mean(ot * ot, axis=0, keepdims=True)
        on = ot * lax.rsqrt(ms + EPS) * gn_ref[...] * (1.0 - lam_init)
        o_ref[0, c0:c0 + tf, :] = on.T.astype(o_ref.dtype)


def _diff_attn(lam_p, gn_col, dq, dk, dv, *, lam_init):
    b, s, w = dq.shape
    nh = w // LANES
    head = lambda bi, hi: (bi, 0, hi)
    const = lambda bi, hi: (0, 0)
    return pl.pallas_call(
        functools.partial(_diff_attn_kernel, lam_init=lam_init),
        grid=(b, nh),
        in_specs=[
            pl.BlockSpec((4, DIFF_HEAD_DIM), const),
            pl.BlockSpec((LANES, 1), const),
            pl.BlockSpec((1, s, LANES), head),
            pl.BlockSpec((1, s, LANES), head),
            pl.BlockSpec((1, s, LANES), head),
        ],
        out_specs=pl.BlockSpec((1, s, LANES), head),
        out_shape=jax.ShapeDtypeStruct((b, s, w), BF16),
        scratch_shapes=[
            pltpu.VMEM((LANES + ATTN_ONES_ROWS, 2 * s), F32),
            pltpu.VMEM((1, 2 * s), F32),
        ],
        compiler_params=_params(("arbitrary", "arbitrary")),
        name="diff_attn",
    )(lam_p, gn_col, dq, dk, dv)


def _seg_scan(x, seg_off, seg_len, op, fill):
    d = 1
    while d < seg_len:
        x = op(x, jnp.where(seg_off >= d, pltpu.roll(x, d, 1), fill))
        d *= 2
    return x


def _log_sigmoid(x):
    return -(jnp.maximum(-x, 0.0) + jnp.log1p(jnp.exp(-jnp.abs(x))))


def _split3(x):
    hi = x.astype(BF16).astype(F32)
    r = x - hi
    mid = r.astype(BF16).astype(F32)
    lo = (r - mid).astype(BF16).astype(F32)
    return hi, mid, lo


GP_W_INTER, GP_FLOOR, GP_W, GP_DECAY, GP_A, GP_NEG_U, GP_COUNT = 0, 1, 2, 3, 4, 7, 10


def _mlstm_gates_kernel(gate_ref, bias_ref, out_ref):
    rows, s = gate_ref.shape
    nseq = rows // SUBLANES
    lc = MLSTM_CHUNK
    nh = N_MLSTM_HEADS
    g = gate_ref[...] + bias_ref[...]
    logf = _log_sigmoid(pltpu.roll(g, rows - nh, 0))
    seg = lax.broadcasted_iota(jnp.int32, (1, s), 1) % lc
    bcum = _seg_scan(logf, seg, lc, lambda x, y: x + y, 0.0)
    a = g - bcum
    cmax = _seg_scan(a, seg, lc, jnp.maximum, -jnp.inf)
    m_prev = jnp.zeros((rows, 1), F32)
    for c in range(s // lc):
        blk = slice(c * lc, (c + 1) * lc)
        a_c, b_c = a[:, blk], bcum[:, blk]
        g_c = b_c[:, lc - 1:lc]
        m_cur = jnp.maximum(g_c + m_prev, g_c + jnp.max(a_c, axis=1, keepdims=True))
        u = jnp.maximum(m_prev, cmax[:, blk])
        planes = [jnp.exp(m_prev - u),
                  jnp.exp(-(u + b_c)),
                  jnp.exp(g_c + a_c - m_cur),
                  jnp.broadcast_to(jnp.exp(g_c + m_prev - m_cur), (rows, lc))]
        planes += list(_split3(a_c * LOG2E)) + list(_split3(-u * LOG2E))
        for k, pv in enumerate(planes):
            for bi in range(nseq):
                out_ref[bi, c, k * SUBLANES:(k + 1) * SUBLANES, :] = (
                    pv[bi * SUBLANES:(bi + 1) * SUBLANES, :])
        m_prev = m_cur


def _mlstm_gates(gates2, bias2, *, batch, seq):
    lc = MLSTM_CHUNK
    gb = MLSTM_GATE_SEQS
    return pl.pallas_call(
        _mlstm_gates_kernel,
        grid=(batch // gb,),
        in_specs=[
            pl.BlockSpec((gb * SUBLANES, seq), lambda i: (i, 0)),
            pl.BlockSpec((gb * SUBLANES, 1), lambda i: (0, 0)),
        ],
        out_specs=pl.BlockSpec((gb, seq // lc, GP_COUNT * SUBLANES, lc),
                               lambda i: (i, 0, 0, 0)),
        out_shape=jax.ShapeDtypeStruct((batch, seq // lc, GP_COUNT * SUBLANES, lc), F32),
        compiler_params=_params(("arbitrary",)),
        name="mlstm_gates",
    )(gates2, bias2)


def _mlstm_kernel(gp_ref, qm_ref, kq_ref, vt_ref, og_ref, gn_ref, out_ref, ct_ref):
    s = qm_ref.shape[1]
    lc = MLSTM_CHUNK
    nh = N_MLSTM_HEADS
    ct_ref[...] = jnp.zeros(ct_ref.shape, F32)
    ones_rows = jnp.ones((MLSTM_ONES_ROWS, lc), BF16)
    causal = (lax.broadcasted_iota(jnp.int32, (lc, lc), 0)
              <= lax.broadcasted_iota(jnp.int32, (lc, lc), 1))
    nt = (((1,), (1,)), ((), ()))
    tn = (((0,), (0,)), ((), ()))
    gnb = [jnp.broadcast_to(gn_ref[hd * LANES:(hd + 1) * LANES, :], (LANES, lc))
           for hd in range(nh)]
    row = lax.broadcasted_iota(jnp.int32, (SUBLANES, lc), 0)
    pick = [jnp.where(row == hd, 1.0, 0.0) for hd in range(nh)]

    def plane(gp, k, n=1):
        return gp[k * SUBLANES:(k + n) * SUBLANES]

    for c in range(s // lc):
        start = c * lc
        gp = gp_ref[0, c]
        for hd in range(nh):
            sl = slice(hd * LANES, (hd + 1) * LANES)
            qm = qm_ref[0, pl.ds(start, lc), sl]
            kq = kq_ref[0, pl.ds(start, lc), sl]
            vt = jnp.concatenate([vt_ref[0, c, sl, :], ones_rows], axis=0)
            w_inter = plane(gp, GP_W_INTER)[hd:hd + 1]
            floor = plane(gp, GP_FLOOR)[hd:hd + 1]
            w_row = plane(gp, GP_W)[hd:hd + 1]
            decay = plane(gp, GP_DECAY)[hd:hd + 1, 0:1]
            e = pick[hd]
            lhs = jnp.concatenate([plane(gp, GP_A, 3), e, e, e], axis=0).astype(BF16)
            rhs = jnp.concatenate([e, e, e, plane(gp, GP_NEG_U, 3)], axis=0).astype(BF16)
            arg = lax.dot_general(lhs, rhs, tn, preferred_element_type=F32)
            st = lax.dot_general(kq, qm, nt, preferred_element_type=F32)
            pt = (st * jnp.exp2(jnp.where(causal, arg, -jnp.inf))).astype(BF16)
            ct = ct_ref[hd]
            nd = (jnp.dot(vt, pt, preferred_element_type=F32)
                  + w_inter * lax.dot_general(ct.astype(BF16), qm, nt,
                                              preferred_element_type=F32))
            den = jnp.maximum(jnp.abs(nd[LANES:LANES + 1]), floor)
            ht = nd[0:LANES] / den
            ms = jnp.mean(ht * ht, axis=0, keepdims=True)
            hn = ht * lax.rsqrt(ms + EPS) * gnb[hd]
            og = og_ref[0, pl.ds(start, lc), sl].astype(F32)
            out_ref[0, pl.ds(start, lc), sl] = (hn.T * og).astype(out_ref.dtype)
            vw = (vt.astype(F32) * w_row).astype(BF16)
            ct_ref[hd] = decay * ct + jnp.dot(vw, kq, preferred_element_type=F32)


def _mlstm(gplanes, qm, kq, mvt, og, gn_col):
    b, s, w = qm.shape
    lc = MLSTM_CHUNK
    seq = lambda bi: (bi, 0, 0)
    seq4 = lambda bi: (bi, 0, 0, 0)
    return pl.pallas_call(
        _mlstm_kernel,
        grid=(b,),
        in_specs=[
            pl.BlockSpec((1,) + gplanes.shape[1:], seq4),
            pl.BlockSpec((1, s, w), seq),
            pl.BlockSpec((1, s, w), seq),
            pl.BlockSpec((1, s // lc, w, lc), seq4),
            pl.BlockSpec((1, s, w), seq),
            pl.BlockSpec((w, 1), lambda bi: (0, 0)),
        ],
        out_specs=pl.BlockSpec((1, s, w), seq),
        out_shape=jax.ShapeDtypeStruct((b, s, w), BF16),
        scratch_shapes=[
            pltpu.VMEM((N_MLSTM_HEADS, LANES + MLSTM_ONES_ROWS, LANES), F32),
        ],
        compiler_params=_params(("arbitrary",)),
        name="mlstm",
    )(gplanes, qm, kq, mvt, og, gn_col)


def _out_mlp_kernel(x_ref, od_ref, om_ref, wod_ref, wom_ref, gpost_ref, gpre_ref,
                    wu_ref, wdn_ref, gmlp_ref, o_ref):
    mixed = (jnp.dot(od_ref[...], wod_ref[...], preferred_element_type=F32)
             + jnp.dot(om_ref[...], wom_ref[...], preferred_element_type=F32))
    x1 = x_ref[...] + _rms(mixed, gpost_ref[...])
    h = _rms(x1, gpre_ref[...]).astype(BF16)
    dff = wu_ref.shape[1]
    acc = None
    for f in range(dff // MLP_FF_CHUNK):
        sl = slice(f * MLP_FF_CHUNK, (f + 1) * MLP_FF_CHUNK)
        a = jnp.maximum(jnp.dot(h, wu_ref[:, sl], preferred_element_type=F32), 0.0)
        part = jnp.dot((a * a).astype(BF16), wdn_ref[sl, :], preferred_element_type=F32)
        acc = part if acc is None else acc + part
    o_ref[...] = x1 + _rms(acc, gmlp_ref[...])


def _out_mlp(x2, od, om, wod, wom, gpost, gpre, wu, wdn, gmlp):
    t, d = x2.shape
    tm = MLP_ROWS
    row = lambda i: (i, 0)
    const = lambda i: (0, 0)
    resident = lambda shape: pl.BlockSpec(shape, const, pipeline_mode=pl.Buffered(1))
    return pl.pallas_call(
        _out_mlp_kernel,
        grid=(t // tm,),
        in_specs=[
            pl.BlockSpec((tm, d), row),
            pl.BlockSpec((tm, od.shape[1]), row),
            pl.BlockSpec((tm, om.shape[1]), row),
            resident(wod.shape),
            resident(wom.shape),
            pl.BlockSpec((1, d), const),
            pl.BlockSpec((1, d), const),
            resident(wu.shape),
            resident(wdn.shape),
            pl.BlockSpec((1, d), const),
        ],
        out_specs=pl.BlockSpec((tm, d), row),
        out_shape=jax.ShapeDtypeStruct((t, d), F32),
        compiler_params=_params(("arbitrary",)),
        name="out_mlp",
    )(x2, od, om, wod, wom, gpost, gpre, wu, wdn, gmlp)


def _rope_tables(seq):
    d = DIFF_HEAD_DIM
    inv = ROPE_THETA ** (-jnp.arange(0, d, 2, dtype=F32) / d)
    ang = jnp.arange(seq, dtype=jnp.int32).astype(F32)[:, None] * inv[None, :]
    cos = jnp.cos(ang)
    sin = jnp.sin(ang)
    zero = jnp.zeros_like(sin)
    reps = LANES // d
    cos_t = jnp.tile(jnp.concatenate([cos, cos], axis=1), (1, reps))
    slo_t = jnp.tile(jnp.concatenate([-sin, zero], axis=1), (1, reps))
    shi_t = jnp.tile(jnp.concatenate([zero, sin], axis=1), (1, reps))
    return cos_t, slo_t, shi_t


def _layer(x, l, norm_mix_pre, w_in, conv_w, conv_b, b_igate, b_fgate, lambda_q1,
           lambda_k1, lambda_q2, lambda_k2, diff_norm, mlstm_norm, w_out,
           norm_mix_post, norm_mlp_pre, w_up, w_down, norm_mlp_post):
    b, s, d = x.shape
    nh, dqk = N_MLSTM_HEADS, MLSTM_QK_DIM
    x2 = x.reshape(b * s, d)

    w = w_in[l]
    wq = w[:, 1536:1792].reshape(d, nh, dqk)
    wk = w[:, 1792:2048].reshape(d, nh, dqk)
    wqk = jnp.concatenate([wq, wk], axis=2).reshape(d, 2 * nh * dqk)
    wg = jnp.pad(w[:, 3072:3080], ((0, 0), (0, LANES - 2 * nh)))
    w_main = jnp.concatenate([w[:, 0:1536], wqk, w[:, 2048:3072], wg], axis=1).astype(BF16)

    def qk_interleave(v):
        lead = v.shape[:-1]
        q = v[..., :nh * dqk].reshape(lead + (nh, dqk))
        k = v[..., nh * dqk:].reshape(lead + (nh, dqk))
        return jnp.concatenate([q, k], axis=-1).reshape(lead + (2 * nh * dqk,))

    cw = qk_interleave(conv_w[l])
    cb = qk_interleave(conv_b[l])[None, :]
    qs = qk_interleave(jnp.concatenate([jnp.full((nh * dqk,), dqk ** -0.5, F32),
                                        jnp.zeros((nh * dqk,), F32)]))[None, :]
    cos_t, slo_t, shi_t = _rope_tables(s)

    dq, dk, dv, qm, kq, mvt, og, gates = _in_proj(
        x2, norm_mix_pre[l][None, :], w_main, cos_t, slo_t, shi_t, cw, cb, qs,
        batch=b, seq=s)

    lam_p = jnp.stack([lambda_q1[l], lambda_k1[l], lambda_q2[l], lambda_k2[l]]).astype(F32)
    o_diff = _diff_attn(lam_p, diff_norm[l][:, None], dq.reshape(b, s, -1),
                        dk.reshape(b, s, -1), dv.reshape(b, s, -1),
                        lam_init=_lambda_init(l))

    bias = jnp.concatenate([b_igate[l], b_fgate[l]]).astype(F32)[:, None]
    gplanes = _mlstm_gates(gates.reshape(b * SUBLANES, s),
                           jnp.tile(bias, (MLSTM_GATE_SEQS, 1)), batch=b, seq=s)
    o_mlstm = _mlstm(gplanes, qm.reshape(b, s, -1), kq.reshape(b, s, -1), mvt,
                     og.reshape(b, s, -1), mlstm_norm[l].reshape(-1, 1))

    wo = w_out[l].astype(BF16)
    nd = o_diff.shape[-1]
    out = _out_mlp(x2, o_diff.reshape(b * s, -1), o_mlstm.reshape(b * s, -1),
                   wo[:nd], wo[nd:], norm_mix_post[l][None, :], norm_mlp_pre[l][None, :],
                   w_up[l].astype(BF16), w_down[l].astype(BF16), norm_mlp_post[l][None, :])
    return out.reshape(b, s, d)


def kernel(x, norm_mix_pre, w_in, conv_w, conv_b, b_igate, b_fgate, lambda_q1, lambda_k1,
           lambda_q2, lambda_k2, diff_norm, mlstm_norm, w_out, norm_mix_post,
           norm_mlp_pre, w_up, w_down, norm_mlp_post):
    for l in range(w_in.shape[0]):
        x = _layer(x, l, norm_mix_pre, w_in, conv_w, conv_b, b_igate, b_fgate,
                   lambda_q1, lambda_k1, lambda_q2, lambda_k2, diff_norm, mlstm_norm,
                   w_out, norm_mix_post, norm_mlp_pre, w_up, w_down, norm_mlp_post)
    return x
```

```python
import functools
import math

import numpy as np
import jax
import jax.numpy as jnp
from jax import lax
from jax.experimental import pallas as pl
from jax.experimental.pallas import tpu as pltpu

F32 = jnp.float32
BF16 = jnp.bfloat16

N_DIFF_HEADS = 4
DIFF_HEAD_DIM = 64
N_MLSTM_HEADS = 4
MLSTM_QK_DIM = 64
MLSTM_V_DIM = 128
CONV_WIDTH = 4
ROPE_THETA = 10000.0
EPS = 1e-6

LANES = 128
SUBLANES = 8
VMEM_LIMIT_BYTES = 56 * 1024 * 1024

IN_PROJ_ROWS = 1024
LOG2E = math.log2(math.e)
ATTN_ONES_ROWS = 16
ATTN_Q_ROWS = 256
ATTN_KV_ROWS = 512
ATTN_COL_CHUNK = 2048
MLSTM_CHUNK = 256
MLSTM_ONES_ROWS = 16
MLSTM_GATE_SEQS = 8
MLP_ROWS = 512
MLP_FF_CHUNK = 1024


def _lambda_init(layer):
    return 0.8 - 0.6 * math.exp(-0.3 * layer)


def _params(semantics, flags=None):
    return pltpu.CompilerParams(dimension_semantics=semantics,
                                vmem_limit_bytes=VMEM_LIMIT_BYTES, flags=flags)


def _rms(x, g):
    return x * lax.rsqrt(jnp.mean(x * x, axis=-1, keepdims=True) + EPS) * g


def _in_proj_kernel(x_ref, g_ref, w_ref, cos_ref, slo_ref, shi_ref, cw_ref, cb_ref,
                    qs_ref, dq_ref, dk_ref, dv_ref, qm_ref, kq_ref, mvt_ref, og_ref,
                    gate_ref, carry_ref, *, tiles_per_seq):
    tm = x_ref.shape[0]
    i = pl.program_id(0)

    @pl.when(i % tiles_per_seq == 0)
    def _():
        carry_ref[...] = jnp.zeros(carry_ref.shape, F32)

    lc = MLSTM_CHUNK
    half = tm // 2
    rows8 = lax.broadcasted_iota(jnp.int32, (SUBLANES, 1), 0)
    cw = cw_ref[...]

    def project(r0, h, tail):
        rs = slice(r0, r0 + half)

        def proj(lo, hi):
            return jnp.dot(h, w_ref[:, lo:hi], preferred_element_type=F32)

        pre = proj(1536, 2048)
        y = cb_ref[...] + cw[3:4, :] * pre
        for j in range(1, CONV_WIDTH):
            sh = pltpu.roll(pre, j, 0)
            head = jnp.where(rows8 < j, pltpu.roll(tail, j, 0), sh[0:SUBLANES, :])
            sh = jnp.concatenate([head, sh[SUBLANES:, :]], axis=0)
            y = y + cw[3 - j:4 - j, :] * sh
        y = y * jax.nn.sigmoid(y)
        qm_ref[rs, :] = (y * qs_ref[...]).astype(BF16)
        kq_ref[rs, :] = jnp.concatenate(
            [pltpu.roll(y[:, gi * LANES:(gi + 1) * LANES], MLSTM_QK_DIM, 1)
             for gi in range(y.shape[1] // LANES)], axis=1).astype(BF16)

        pmv = proj(2048, 2560)
        for c in range(half // lc):
            mvt_ref[0, r0 // lc + c] = pmv[c * lc:(c + 1) * lc, :].T.astype(BF16)

        gate_ref[0, :, rs] = proj(3072, 3200).T[0:SUBLANES, :]
        og_ref[rs, :] = jax.nn.sigmoid(proj(2560, 3072)).astype(BF16)

        cos = cos_ref[rs, :]
        s_lo = slo_ref[rs, :]
        s_hi = shi_ref[rs, :]

        def rope(p):
            outs = []
            for gi in range(p.shape[1] // LANES):
                v = p[:, gi * LANES:(gi + 1) * LANES]
                outs.append(v * cos + pltpu.roll(v, LANES - 32, 1) * s_lo
                            + pltpu.roll(v, 32, 1) * s_hi)
            return jnp.concatenate(outs, axis=1)

        dq_ref[rs, :] = (rope(proj(0, 512)) * (DIFF_HEAD_DIM ** -0.5 * LOG2E)).astype(BF16)
        dk_ref[rs, :] = rope(proj(512, 1024)).astype(BF16)
        dv_ref[rs, :] = proj(1024, 1536).astype(BF16)
        return pre[half - SUBLANES:half, :]

    h_a = _rms(x_ref[0:half, :], g_ref[...]).astype(BF16)
    h_b = _rms(x_ref[half:tm, :], g_ref[...]).astype(BF16)
    tail_a = project(0, h_a, carry_ref[...])
    carry_ref[...] = project(half, h_b, tail_a)


def _in_proj(x2, g, w_main, cos_t, slo_t, shi_t, cw, cb, qs, *, batch, seq):
    t, d = x2.shape
    tm = IN_PROJ_ROWS
    lc = MLSTM_CHUNK
    tps = seq // tm
    nw = w_main.shape[1]
    row = lambda i: (i, 0)
    const = lambda i: (0, 0)
    pos = lambda i: (i % tps, 0)
    out_bf = jax.ShapeDtypeStruct((t, 512), BF16)
    return pl.pallas_call(
        functools.partial(_in_proj_kernel, tiles_per_seq=tps),
        grid=(t // tm,),
        in_specs=[
            pl.BlockSpec((tm, d), row),
            pl.BlockSpec((1, d), const),
            pl.BlockSpec((d, nw), const, pipeline_mode=pl.Buffered(1)),
            pl.BlockSpec((tm, LANES), pos),
            pl.BlockSpec((tm, LANES), pos),
            pl.BlockSpec((tm, LANES), pos),
            pl.BlockSpec((CONV_WIDTH, 512), const),
            pl.BlockSpec((1, 512), const),
            pl.BlockSpec((1, 512), const),
        ],
        out_specs=[
            pl.BlockSpec((tm, 512), row),
            pl.BlockSpec((tm, 512), row),
            pl.BlockSpec((tm, 512), row),
            pl.BlockSpec((tm, 512), row),
            pl.BlockSpec((tm, 512), row),
            pl.BlockSpec((1, tm // lc, 512, lc), lambda i: (i // tps, i % tps, 0, 0)),
            pl.BlockSpec((tm, 512), row),
            pl.BlockSpec((1, SUBLANES, tm), lambda i: (i // tps, 0, i % tps)),
        ],
        out_shape=[out_bf, out_bf, out_bf, out_bf, out_bf,
                   jax.ShapeDtypeStruct((batch, seq // lc, 512, lc), BF16), out_bf,
                   jax.ShapeDtypeStruct((batch, SUBLANES, seq), F32)],
        scratch_shapes=[pltpu.VMEM((SUBLANES, 512), F32)],
        compiler_params=_params(("arbitrary",)),
        name="in_proj",
    )(x2, g, w_main, cos_t, slo_t, shi_t, cw, cb, qs)


def _diff_attn_kernel(lam_ref, gn_ref, q_ref, k_ref, v_ref, o_ref,
                      acc_ref, m_ref, *, lam_init):
    s = q_ref.shape[1]
    tk = ATTN_KV_ROWS
    tf = ATTN_Q_ROWS
    lp = lam_ref[...]
    lam = (jnp.exp(jnp.sum(lp[0:1] * lp[1:2], axis=-1, keepdims=True))
           - jnp.exp(jnp.sum(lp[2:3] * lp[3:4], axis=-1, keepdims=True)) + lam_init)

    lane = lax.broadcasted_iota(jnp.int32, (1, LANES), 1)
    nt = (((1,), (1,)), ((), ()))
    keep = (lax.broadcasted_iota(jnp.int32, (tk, tk), 1)
            >= lax.broadcasted_iota(jnp.int32, (tk, tk), 0))
    q = q_ref[0]
    zero = jnp.zeros_like(q)
    qmaps = (jnp.where(lane < DIFF_HEAD_DIM, q, zero),
             jnp.where(lane >= DIFF_HEAD_DIM, q, zero))

    ones_rows = jnp.ones((ATTN_ONES_ROWS, tk), BF16)

    cw = ATTN_COL_CHUNK
    units = [(j, mi, c0, min(c0 + cw, s))
             for j in range(s // tk) for c0 in range(j * tk, s, cw) for mi in range(2)]

    def scores(unit):
        j, mi, c0, c1 = unit
        kb = k_ref[0, j * tk:(j + 1) * tk, :]
        st = lax.dot_general(kb, qmaps[mi][c0:c1, :], nt, preferred_element_type=F32)
        if c0 == j * tk:
            diag = jnp.where(keep, st[:, :tk], -jnp.inf)
            st = diag if c1 - c0 == tk else jnp.concatenate([diag, st[:, tk:]], axis=1)
        return st

    vts = {}

    def vt_aug(j):
        if j not in vts:
            lo = j * tk
            vts[j] = jnp.concatenate(
                [jnp.concatenate(
                    [v_ref[0, lo + c * LANES:lo + (c + 1) * LANES, :].astype(F32).T
                     for c in range(tk // LANES)], axis=1).astype(BF16), ones_rows], axis=0)
        return vts[j]

    st_next = scores(units[0])
    for u, (j, mi, c0, c1) in enumerate(units):
        st = st_next
        if u + 1 < len(units):
            st_next = scores(units[u + 1])
        cols = slice(mi * s + c0, mi * s + c1)
        m_blk = jnp.max(st, axis=0, keepdims=True)
        if j == 0:
            m_ref[:, cols] = m_blk
            acc_ref[:, cols] = jnp.dot(vt_aug(j), jnp.exp2(st - m_blk).astype(BF16),
                                       preferred_element_type=F32)
            continue
        m_old = m_ref[:, cols]
        m_new = jnp.maximum(m_old, m_blk)
        alpha = jnp.exp2(m_old - m_new)
        p = jnp.exp2(st - m_new).astype(BF16)
        m_ref[:, cols] = m_new
        acc_ref[:, cols] = alpha * acc_ref[:, cols] + jnp.dot(
            vt_aug(j), p, preferred_element_type=F32)

    for c0 in range(0, s, tf):
        a1 = acc_ref[0:LANES, c0:c0 + tf] / acc_ref[LANES:LANES + 1, c0:c0 + tf]
        a2 = (acc_ref[0:LANES, s + c0:s + c0 + tf]
              / acc_ref[LANES:LANES + 1, s + c0:s + c0 + tf])
        ot = a1 - lam * a2
        ms = jnp.mean(ot * ot, axis=0, keepdims=True)
        on = ot * lax.rsqrt(ms + EPS) * gn_ref[...] * (1.0 - lam_init)
        o_ref[0, c0:c0 + tf, :] = on.T.astype(o_ref.dtype)


def _diff_attn(lam_p, gn_col, dq, dk, dv, *, lam_init):
    b, s, w = dq.shape
    nh = w // LANES
    head = lambda bi, hi: (bi, 0, hi)
    const = lambda bi, hi: (0, 0)
    return pl.pallas_call(
        functools.partial(_diff_attn_kernel, lam_init=lam_init),
        grid=(b, nh),
        in_specs=[
            pl.BlockSpec((4, DIFF_HEAD_DIM), const),
            pl.BlockSpec((LANES, 1), const),
            pl.BlockSpec((1, s, LANES), head),
            pl.BlockSpec((1, s, LANES), head),
            pl.BlockSpec((1, s, LANES), head),
        ],
        out_specs=pl.BlockSpec((1, s, LANES), head),
        out_shape=jax.ShapeDtypeStruct((b, s, w), BF16),
        scratch_shapes=[
            pltpu.VMEM((LANES + ATTN_ONES_ROWS, 2 * s), F32),
            pltpu.VMEM((1, 2 * s), F32),
        ],
        compiler_params=_params(("arbitrary", "arbitrary")),
        name="diff_attn",
    )(lam_p, gn_col, dq, dk, dv)


def _seg_scan(x, seg_off, seg_len, op, fill):
    d = 1
    while d < seg_len:
        x = op(x, jnp.where(seg_off >= d, pltpu.roll(x, d, 1), fill))
        d *= 2
    return x


def _log_sigmoid(x):
    return -(jnp.maximum(-x, 0.0) + jnp.log1p(jnp.exp(-jnp.abs(x))))


def _split3(x):
    hi = x.astype(BF16).astype(F32)
    r = x - hi
    mid = r.astype(BF16).astype(F32)
    lo = (r - mid).astype(BF16).astype(F32)
    return hi, mid, lo


GP_W_INTER, GP_FLOOR, GP_W, GP_DECAY, GP_A, GP_NEG_U, GP_COUNT = 0, 1, 2, 3, 4, 7, 10


def _mlstm_gates_kernel(gate_ref, bias_ref, out_ref):
    rows, s = gate_ref.shape
    nseq = rows // SUBLANES
    lc = MLSTM_CHUNK
    nh = N_MLSTM_HEADS
    g = gate_ref[...] + bias_ref[...]
    logf = _log_sigmoid(pltpu.roll(g, rows - nh, 0))
    seg = lax.broadcasted_iota(jnp.int32, (1, s), 1) % lc
    bcum = _seg_scan(logf, seg, lc, lambda x, y: x + y, 0.0)
    a = g - bcum
    cmax = _seg_scan(a, seg, lc, jnp.maximum, -jnp.inf)
    m_prev = jnp.zeros((rows, 1), F32)
    for c in range(s // lc):
        blk = slice(c * lc, (c + 1) * lc)
        a_c, b_c = a[:, blk], bcum[:, blk]
        g_c = b_c[:, lc - 1:lc]
        m_cur = jnp.maximum(g_c + m_prev, g_c + jnp.max(a_c, axis=1, keepdims=True))
        u = jnp.maximum(m_prev, cmax[:, blk])
        planes = [jnp.exp(m_prev - u),
                  jnp.exp(-(u + b_c)),
                  jnp.exp(g_c + a_c - m_cur),
                  jnp.broadcast_to(jnp.exp(g_c + m_prev - m_cur), (rows, lc))]
        planes += list(_split3(a_c * LOG2E)) + list(_split3(-u * LOG2E))
        for k, pv in enumerate(planes):
            for bi in range(nseq):
                out_ref[bi, c, k * SUBLANES:(k + 1) * SUBLANES, :] = (
                    pv[bi * SUBLANES:(bi + 1) * SUBLANES, :])
        m_prev = m_cur


def _mlstm_gates(gates2, bias2, *, batch, seq):
    lc = MLSTM_CHUNK
    gb = MLSTM_GATE_SEQS
    return pl.pallas_call(
        _mlstm_gates_kernel,
        grid=(batch // gb,),
        in_specs=[
            pl.BlockSpec((gb * SUBLANES, seq), lambda i: (i, 0)),
            pl.BlockSpec((gb * SUBLANES, 1), lambda i: (0, 0)),
        ],
        out_specs=pl.BlockSpec((gb, seq // lc, GP_COUNT * SUBLANES, lc),
                               lambda i: (i, 0, 0, 0)),
        out_shape=jax.ShapeDtypeStruct((batch, seq // lc, GP_COUNT * SUBLANES, lc), F32),
        compiler_params=_params(("arbitrary",)),
        name="mlstm_gates",
    )(gates2, bias2)


def _mlstm_kernel(gp_ref, qm_ref, kq_ref, vt_ref, og_ref, gn_ref, out_ref, ct_ref):
    s = qm_ref.shape[1]
    lc = MLSTM_CHUNK
    nh = N_MLSTM_HEADS
    ct_ref[...] = jnp.zeros(ct_ref.shape, F32)
    ones_rows = jnp.ones((MLSTM_ONES_ROWS, lc), BF16)
    causal = (lax.broadcasted_iota(jnp.int32, (lc, lc), 0)
              <= lax.broadcasted_iota(jnp.int32, (lc, lc), 1))
    nt = (((1,), (1,)), ((), ()))
    tn = (((0,), (0,)), ((), ()))
    gnb = [jnp.broadcast_to(gn_ref[hd * LANES:(hd + 1) * LANES, :], (LANES, lc))
           for hd in range(nh)]
    row = lax.broadcasted_iota(jnp.int32, (SUBLANES, lc), 0)
    pick = [jnp.where(row == hd, 1.0, 0.0) for hd in range(nh)]

    def plane(gp, k, n=1):
        return gp[k * SUBLANES:(k + n) * SUBLANES]

    def intra(c, hd):
        gp = gp_ref[0, c]
        sl = slice(hd * LANES, (hd + 1) * LANES)
        qm = qm_ref[0, c * lc:(c + 1) * lc, sl]
        kq = kq_ref[0, c * lc:(c + 1) * lc, sl]
        e = pick[hd]
        lhs = jnp.concatenate([plane(gp, GP_A, 3), e, e, e], axis=0).astype(BF16)
        rhs = jnp.concatenate([e, e, e, plane(gp, GP_NEG_U, 3)], axis=0).astype(BF16)
        arg = lax.dot_general(lhs, rhs, tn, preferred_element_type=F32)
        st = lax.dot_general(kq, qm, nt, preferred_element_type=F32)
        return (st * jnp.exp2(jnp.where(causal, arg, -jnp.inf))).astype(BF16)

    units = [(c, hd) for c in range(s // lc) for hd in range(nh)]
    pt_next = intra(*units[0])
    for ui, (c, hd) in enumerate(units):
        pt = pt_next
        if ui + 1 < len(units):
            pt_next = intra(*units[ui + 1])
        start = c * lc
        gp = gp_ref[0, c]
        sl = slice(hd * LANES, (hd + 1) * LANES)
        qm = qm_ref[0, pl.ds(start, lc), sl]
        kq = kq_ref[0, pl.ds(start, lc), sl]
        vt = jnp.concatenate([vt_ref[0, c, sl, :], ones_rows], axis=0)
        w_inter = plane(gp, GP_W_INTER)[hd:hd + 1]
        floor = plane(gp, GP_FLOOR)[hd:hd + 1]
        w_row = plane(gp, GP_W)[hd:hd + 1]
        decay = plane(gp, GP_DECAY)[hd:hd + 1, 0:1]
        ct = ct_ref[hd]
        nd = (jnp.dot(vt, pt, preferred_element_type=F32)
              + w_inter * lax.dot_general(ct.astype(BF16), qm, nt,
                                          preferred_element_type=F32))
        den = jnp.maximum(jnp.abs(nd[LANES:LANES + 1]), floor)
        ht = nd[0:LANES] / den
        ms = jnp.mean(ht * ht, axis=0, keepdims=True)
        hn = ht * lax.rsqrt(ms + EPS) * gnb[hd]
        og = og_ref[0, pl.ds(start, lc), sl].astype(F32)
        out_ref[0, pl.ds(start, lc), sl] = (hn.T * og).astype(out_ref.dtype)
        vw = (vt.astype(F32) * w_row).astype(BF16)
        ct_ref[hd] = decay * ct + jnp.dot(vw, kq, preferred_element_type=F32)


def _mlstm(gplanes, qm, kq, mvt, og, gn_col):
    b, s, w = qm.shape
    lc = MLSTM_CHUNK
    seq = lambda bi: (bi, 0, 0)
    seq4 = lambda bi: (bi, 0, 0, 0)
    return pl.pallas_call(
        _mlstm_kernel,
        grid=(b,),
        in_specs=[
            pl.BlockSpec((1,) + gplanes.shape[1:], seq4),
            pl.BlockSpec((1, s, w), seq),
            pl.BlockSpec((1, s, w), seq),
            pl.BlockSpec((1, s // lc, w, lc), seq4),
            pl.BlockSpec((1, s, w), seq),
            pl.BlockSpec((w, 1), lambda bi: (0, 0)),
        ],
        out_specs=pl.BlockSpec((1, s, w), seq),
        out_shape=jax.ShapeDtypeStruct((b, s, w), BF16),
        scratch_shapes=[
            pltpu.VMEM((N_MLSTM_HEADS, LANES + MLSTM_ONES_ROWS, LANES), F32),
        ],
        compiler_params=_params(("arbitrary",)),
        name="mlstm",
    )(gplanes, qm, kq, mvt, og, gn_col)


def _out_mlp_kernel(x_ref, od_ref, om_ref, wod_ref, wom_ref, gpost_ref, gpre_ref,
                    wu_ref, wdn_ref, gmlp_ref, o_ref):
    half = x_ref.shape[0] // 2
    rows = (slice(0, half), slice(half, 2 * half))

    def mix(rs):
        mixed = (jnp.dot(od_ref[rs, :], wod_ref[...], preferred_element_type=F32)
                 + jnp.dot(om_ref[rs, :], wom_ref[...], preferred_element_type=F32))
        x1 = x_ref[rs, :] + _rms(mixed, gpost_ref[...])
        return x1, _rms(x1, gpre_ref[...]).astype(BF16)

    def mlp(h):
        acc = None
        for f in range(wu_ref.shape[1] // MLP_FF_CHUNK):
            sl = slice(f * MLP_FF_CHUNK, (f + 1) * MLP_FF_CHUNK)
            a = jnp.maximum(jnp.dot(h, wu_ref[:, sl], preferred_element_type=F32), 0.0)
            part = jnp.dot((a * a).astype(BF16), wdn_ref[sl, :], preferred_element_type=F32)
            acc = part if acc is None else acc + part
        return acc

    xa, ha = mix(rows[0])
    xb, hb = mix(rows[1])
    acc_a = mlp(ha)
    acc_b = mlp(hb)
    o_ref[rows[0], :] = xa + _rms(acc_a, gmlp_ref[...])
    o_ref[rows[1], :] = xb + _rms(acc_b, gmlp_ref[...])


def _out_mlp(x2, od, om, wod, wom, gpost, gpre, wu, wdn, gmlp):
    t, d = x2.shape
    tm = MLP_ROWS
    row = lambda i: (i, 0)
    const = lambda i: (0, 0)
    resident = lambda shape: pl.BlockSpec(shape, const, pipeline_mode=pl.Buffered(1))
    return pl.pallas_call(
        _out_mlp_kernel,
        grid=(t // tm,),
        in_specs=[
            pl.BlockSpec((tm, d), row),
            pl.BlockSpec((tm, od.shape[1]), row),
            pl.BlockSpec((tm, om.shape[1]), row),
            resident(wod.shape),
            resident(wom.shape),
            pl.BlockSpec((1, d), const),
            pl.BlockSpec((1, d), const),
            resident(wu.shape),
            resident(wdn.shape),
            pl.BlockSpec((1, d), const),
        ],
        out_specs=pl.BlockSpec((tm, d), row),
        out_shape=jax.ShapeDtypeStruct((t, d), F32),
        compiler_params=_params(("arbitrary",)),
        name="out_mlp",
    )(x2, od, om, wod, wom, gpost, gpre, wu, wdn, gmlp)


def _rope_tables(seq):
    d = DIFF_HEAD_DIM
    inv = ROPE_THETA ** (-jnp.arange(0, d, 2, dtype=F32) / d)
    ang = jnp.arange(seq, dtype=jnp.int32).astype(F32)[:, None] * inv[None, :]
    cos = jnp.cos(ang)
    sin = jnp.sin(ang)
    zero = jnp.zeros_like(sin)
    reps = LANES // d
    cos_t = jnp.tile(jnp.concatenate([cos, cos], axis=1), (1, reps))
    slo_t = jnp.tile(jnp.concatenate([-sin, zero], axis=1), (1, reps))
    shi_t = jnp.tile(jnp.concatenate([zero, sin], axis=1), (1, reps))
    return cos_t, slo_t, shi_t


def _layer(x, l, norm_mix_pre, w_in, conv_w, conv_b, b_igate, b_fgate, lambda_q1,
           lambda_k1, lambda_q2, lambda_k2, diff_norm, mlstm_norm, w_out,
           norm_mix_post, norm_mlp_pre, w_up, w_down, norm_mlp_post):
    b, s, d = x.shape
    nh, dqk = N_MLSTM_HEADS, MLSTM_QK_DIM
    x2 = x.reshape(b * s, d)

    w = w_in[l]
    wq = w[:, 1536:1792].reshape(d, nh, dqk)
    wk = w[:, 1792:2048].reshape(d, nh, dqk)
    wqk = jnp.concatenate([wq, wk], axis=2).reshape(d, 2 * nh * dqk)
    wg = jnp.pad(w[:, 3072:3080], ((0, 0), (0, LANES - 2 * nh)))
    w_main = jnp.concatenate([w[:, 0:1536], wqk, w[:, 2048:3072], wg], axis=1).astype(BF16)

    def qk_interleave(v):
        lead = v.shape[:-1]
        q = v[..., :nh * dqk].reshape(lead + (nh, dqk))
        k = v[..., nh * dqk:].reshape(lead + (nh, dqk))
        return jnp.concatenate([q, k], axis=-1).reshape(lead + (2 * nh * dqk,))

    cw = qk_interleave(conv_w[l])
    cb = qk_interleave(conv_b[l])[None, :]
    qs = qk_interleave(jnp.concatenate([jnp.full((nh * dqk,), dqk ** -0.5, F32),
                                        jnp.zeros((nh * dqk,), F32)]))[None, :]
    cos_t, slo_t, shi_t = _rope_tables(s)

    dq, dk, dv, qm, kq, mvt, og, gates = _in_proj(
        x2, norm_mix_pre[l][None, :], w_main, cos_t, slo_t, shi_t, cw, cb, qs,
        batch=b, seq=s)

    lam_p = jnp.stack([lambda_q1[l], lambda_k1[l], lambda_q2[l], lambda_k2[l]]).astype(F32)
    o_diff = _diff_attn(lam_p, diff_norm[l][:, None], dq.reshape(b, s, -1),
                        dk.reshape(b, s, -1), dv.reshape(b, s, -1),
                        lam_init=_lambda_init(l))

    bias = jnp.concatenate([b_igate[l], b_fgate[l]]).astype(F32)[:, None]
    gplanes = _mlstm_gates(gates.reshape(b * SUBLANES, s),
                           jnp.tile(bias, (MLSTM_GATE_SEQS, 1)), batch=b, seq=s)
    o_mlstm = _mlstm(gplanes, qm.reshape(b, s, -1), kq.reshape(b, s, -1), mvt,
                     og.reshape(b, s, -1), mlstm_norm[l].reshape(-1, 1))

    wo = w_out[l].astype(BF16)
    nd = o_diff.shape[-1]
    out = _out_mlp(x2, o_diff.reshape(b * s, -1), o_mlstm.reshape(b * s, -1),
                   wo[:nd], wo[nd:], norm_mix_post[l][None, :], norm_mlp_pre[l][None, :],
                   w_up[l].astype(BF16), w_down[l].astype(BF16), norm_mlp_post[l][None, :])
    return out.reshape(b, s, d)


def kernel(x, norm_mix_pre, w_in, conv_w, conv_b, b_igate, b_fgate, lambda_q1, lambda_k1,
           lambda_q2, lambda_k2, diff_norm, mlstm_norm, w_out, norm_mix_post,
           norm_mlp_pre, w_up, w_down, norm_mlp_post):
    for l in range(w_in.shape[0]):
        x = _layer(x, l, norm_mix_pre, w_in, conv_w, conv_b, b_igate, b_fgate,
                   lambda_q1, lambda_k1, lambda_q2, lambda_k2, diff_norm, mlstm_norm,
                   w_out, norm_mix_post, norm_mlp_pre, w_up, w_down, norm_mlp_post)
    return x
```

```python
import functools
import math

import numpy as np
import jax
import jax.numpy as jnp
from jax import lax
from jax.experimental import pallas as pl
from jax.experimental.pallas import tpu as pltpu

F32 = jnp.float32
BF16 = jnp.bfloat16

N_DIFF_HEADS = 4
DIFF_HEAD_DIM = 64
N_MLSTM_HEADS = 4
MLSTM_QK_DIM = 64
MLSTM_V_DIM = 128
CONV_WIDTH = 4
ROPE_THETA = 10000.0
EPS = 1e-6

LANES = 128
SUBLANES = 8
VMEM_LIMIT_BYTES = 56 * 1024 * 1024

IN_PROJ_ROWS = 1024
IN_PROJ_AHEAD = 2
LOG2E = math.log2(math.e)
ATTN_ONES_ROWS = 16
ATTN_Q_ROWS = 256
ATTN_KV_ROWS = 512
ATTN_SCORES_AHEAD = 2
MLSTM_CHUNK = 256
MLSTM_ONES_ROWS = 16
MLSTM_GATE_SEQS = 8
MLSTM_SCORES_AHEAD = 2
MLP_ROWS = 512
MLP_FF_CHUNK = 1024


def _lambda_init(layer):
    return 0.8 - 0.6 * math.exp(-0.3 * layer)


def _params(semantics, flags=None):
    return pltpu.CompilerParams(dimension_semantics=semantics,
                                vmem_limit_bytes=VMEM_LIMIT_BYTES, flags=flags)


def _rms(x, g):
    return x * lax.rsqrt(jnp.mean(x * x, axis=-1, keepdims=True) + EPS) * g


def _in_proj_kernel(x_ref, g_ref, w_ref, cos_ref, slo_ref, shi_ref, cw_ref, cb_ref,
                    qs_ref, dq_ref, dk_ref, dv_ref, qm_ref, kq_ref, mvt_ref, og_ref,
                    gate_ref, carry_ref, *, tiles_per_seq):
    tm = x_ref.shape[0]
    i = pl.program_id(0)

    @pl.when(i % tiles_per_seq == 0)
    def _():
        carry_ref[...] = jnp.zeros(carry_ref.shape, F32)

    lc = MLSTM_CHUNK
    half = tm // 2
    rows8 = lax.broadcasted_iota(jnp.int32, (SUBLANES, 1), 0)
    cw = cw_ref[...]

    def project(r0, h, tail):
        rs = slice(r0, r0 + half)

        def conv_qk(pre):
            y = cb_ref[...] + cw[3:4, :] * pre
            for j in range(1, CONV_WIDTH):
                sh = pltpu.roll(pre, j, 0)
                head = jnp.where(rows8 < j, pltpu.roll(tail, j, 0), sh[0:SUBLANES, :])
                sh = jnp.concatenate([head, sh[SUBLANES:, :]], axis=0)
                y = y + cw[3 - j:4 - j, :] * sh
            y = y * jax.nn.sigmoid(y)
            qm_ref[rs, :] = (y * qs_ref[...]).astype(BF16)
            kq_ref[rs, :] = jnp.concatenate(
                [pltpu.roll(y[:, gi * LANES:(gi + 1) * LANES], MLSTM_QK_DIM, 1)
                 for gi in range(y.shape[1] // LANES)], axis=1).astype(BF16)

        def values_t(pmv):
            for c in range(half // lc):
                mvt_ref[0, r0 // lc + c] = pmv[c * lc:(c + 1) * lc, :].T.astype(BF16)

        def gates(pg):
            gate_ref[0, :, rs] = pg.T[0:SUBLANES, :]

        def out_gate(po):
            og_ref[rs, :] = jax.nn.sigmoid(po).astype(BF16)

        def rope(p):
            cos, s_lo, s_hi = cos_ref[rs, :], slo_ref[rs, :], shi_ref[rs, :]
            outs = []
            for gi in range(p.shape[1] // LANES):
                v = p[:, gi * LANES:(gi + 1) * LANES]
                outs.append(v * cos + pltpu.roll(v, LANES - 32, 1) * s_lo
                            + pltpu.roll(v, 32, 1) * s_hi)
            return jnp.concatenate(outs, axis=1)

        def attn_q(p):
            dq_ref[rs, :] = (rope(p) * (DIFF_HEAD_DIM ** -0.5 * LOG2E)).astype(BF16)

        def attn_k(p):
            dk_ref[rs, :] = rope(p).astype(BF16)

        def attn_v(p):
            dv_ref[rs, :] = p.astype(BF16)

        segments = [((1536, 2048), conv_qk), ((2048, 2560), values_t), ((3072, 3200), gates),
                    ((2560, 3072), out_gate), ((0, 512), attn_q), ((512, 1024), attn_k),
                    ((1024, 1536), attn_v)]
        proj = lambda cols: jnp.dot(h, w_ref[:, cols[0]:cols[1]], preferred_element_type=F32)
        ahead = [proj(cols) for cols, _ in segments[:IN_PROJ_AHEAD]]
        pre = ahead[0]
        for si, (_, epilogue) in enumerate(segments):
            p = ahead.pop(0)
            if si + IN_PROJ_AHEAD < len(segments):
                ahead.append(proj(segments[si + IN_PROJ_AHEAD][0]))
            epilogue(p)
        return pre[half - SUBLANES:half, :]

    h_a = _rms(x_ref[0:half, :], g_ref[...]).astype(BF16)
    h_b = _rms(x_ref[half:tm, :], g_ref[...]).astype(BF16)
    tail_a = project(0, h_a, carry_ref[...])
    carry_ref[...] = project(half, h_b, tail_a)


def _in_proj(x2, g, w_main, cos_t, slo_t, shi_t, cw, cb, qs, *, batch, seq):
    t, d = x2.shape
    tm = IN_PROJ_ROWS
    lc = MLSTM_CHUNK
    tps = seq // tm
    nw = w_main.shape[1]
    row = lambda i: (i, 0)
    const = lambda i: (0, 0)
    pos = lambda i: (i % tps, 0)
    out_bf = jax.ShapeDtypeStruct((t, 512), BF16)
    return pl.pallas_call(
        functools.partial(_in_proj_kernel, tiles_per_seq=tps),
        grid=(t // tm,),
        in_specs=[
            pl.BlockSpec((tm, d), row),
            pl.BlockSpec((1, d), const),
            pl.BlockSpec((d, nw), const, pipeline_mode=pl.Buffered(1)),
            pl.BlockSpec((tm, LANES), pos),
            pl.BlockSpec((tm, LANES), pos),
            pl.BlockSpec((tm, LANES), pos),
            pl.BlockSpec((CONV_WIDTH, 512), const),
            pl.BlockSpec((1, 512), const),
            pl.BlockSpec((1, 512), const),
        ],
        out_specs=[
            pl.BlockSpec((tm, 512), row),
            pl.BlockSpec((tm, 512), row),
            pl.BlockSpec((tm, 512), row),
            pl.BlockSpec((tm, 512), row),
            pl.BlockSpec((tm, 512), row),
            pl.BlockSpec((1, tm // lc, 512, lc), lambda i: (i // tps, i % tps, 0, 0)),
            pl.BlockSpec((tm, 512), row),
            pl.BlockSpec((1, SUBLANES, tm), lambda i: (i // tps, 0, i % tps)),
        ],
        out_shape=[out_bf, out_bf, out_bf, out_bf, out_bf,
                   jax.ShapeDtypeStruct((batch, seq // lc, 512, lc), BF16), out_bf,
                   jax.ShapeDtypeStruct((batch, SUBLANES, seq), F32)],
        scratch_shapes=[pltpu.VMEM((SUBLANES, 512), F32)],
        compiler_params=_params(("arbitrary",)),
        name="in_proj",
    )(x2, g, w_main, cos_t, slo_t, shi_t, cw, cb, qs)


def _diff_attn_kernel(lam_ref, gn_ref, q_ref, k_ref, v_ref, o_ref,
                      acc_ref, m_ref, *, lam_init):
    s = q_ref.shape[1]
    tk = ATTN_KV_ROWS
    tf = ATTN_Q_ROWS
    lp = lam_ref[...]
    lam = (jnp.exp(jnp.sum(lp[0:1] * lp[1:2], axis=-1, keepdims=True))
           - jnp.exp(jnp.sum(lp[2:3] * lp[3:4], axis=-1, keepdims=True)) + lam_init)

    lane = lax.broadcasted_iota(jnp.int32, (1, LANES), 1)
    nt = (((1,), (1,)), ((), ()))
    q = q_ref[0]
    zero = jnp.zeros_like(q)
    qmaps = (jnp.where(lane < DIFF_HEAD_DIM, q, zero),
             jnp.where(lane >= DIFF_HEAD_DIM, q, zero))

    hk = tk // 2
    units = []
    for lo in range(0, s, tk):
        units += [(lo, lo + hk, lo, lo + hk), (lo, lo + tk, lo + hk, s)]

    def scores(unit):
        k0, k1, c0, c1 = unit
        n = c1 - c0
        qq = jnp.concatenate([qmaps[0][c0:c1, :], qmaps[1][c0:c1, :]], axis=0)
        st = lax.dot_general(k_ref[0, k0:k1, :], qq, nt,
                             preferred_element_type=F32)
        nmask = k1 - c0
        keep = (lax.broadcasted_iota(jnp.int32, (k1 - k0, nmask), 1) + (c0 - k0)
                >= lax.broadcasted_iota(jnp.int32, (k1 - k0, nmask), 0))
        parts = []
        for base in (0, n):
            parts.append(jnp.where(keep, st[:, base:base + nmask], -jnp.inf))
            if n > nmask:
                parts.append(st[:, base + nmask:base + n])
        return jnp.concatenate(parts, axis=1)

    def get(ref, c0, c1):
        return jnp.concatenate([ref[:, c0:c1], ref[:, s + c0:s + c1]], axis=1)

    def put(ref, c0, c1, val):
        n = c1 - c0
        ref[:, c0:c1] = val[:, :n]
        ref[:, s + c0:s + c1] = val[:, n:]

    vt_tiles = {}

    def vt_tile(r):
        if r not in vt_tiles:
            vt_tiles[r] = v_ref[0, r:r + LANES, :].astype(F32).T.astype(BF16)
        return vt_tiles[r]

    def vt_aug(k0, k1):
        return jnp.concatenate(
            [jnp.concatenate([vt_tile(r) for r in range(k0, k1, LANES)], axis=1),
             jnp.ones((ATTN_ONES_ROWS, k1 - k0), BF16)], axis=0)

    ahead = [scores(un) for un in units[:ATTN_SCORES_AHEAD]]
    for u, (k0, k1, c0, c1) in enumerate(units):
        st = ahead.pop(0)
        if u + ATTN_SCORES_AHEAD < len(units):
            ahead.append(scores(units[u + ATTN_SCORES_AHEAD]))
        m_blk = jnp.max(st, axis=0, keepdims=True)
        if k0 == 0:
            put(m_ref, c0, c1, m_blk)
            put(acc_ref, c0, c1, jnp.dot(vt_aug(k0, k1), jnp.exp2(st - m_blk).astype(BF16),
                                         preferred_element_type=F32))
            continue
        m_old = get(m_ref, c0, c1)
        m_new = jnp.maximum(m_old, m_blk)
        alpha = jnp.exp2(m_old - m_new)
        p = jnp.exp2(st - m_new).astype(BF16)
        put(m_ref, c0, c1, m_new)
        put(acc_ref, c0, c1, alpha * get(acc_ref, c0, c1) + jnp.dot(
            vt_aug(k0, k1), p, preferred_element_type=F32))

    for c0 in range(0, s, tf):
        a1 = acc_ref[0:LANES, c0:c0 + tf] / acc_ref[LANES:LANES + 1, c0:c0 + tf]
        a2 = (acc_ref[0:LANES, s + c0:s + c0 + tf]
              / acc_ref[LANES:LANES + 1, s + c0:s + c0 + tf])
        ot = a1 - lam * a2
        ms = jnp.mean(ot * ot, axis=0, keepdims=True)
        on = ot * lax.rsqrt(ms + EPS) * gn_ref[...] * (1.0 - lam_init)
        o_ref[0, c0:c0 + tf, :] = on.T.astype(o_ref.dtype)


def _diff_attn(lam_p, gn_col, dq, dk, dv, *, lam_init):
    b, s, w = dq.shape
    nh = w // LANES
    head = lambda bi, hi: (bi, 0, hi)
    const = lambda bi, hi: (0, 0)
    return pl.pallas_call(
        functools.partial(_diff_attn_kernel, lam_init=lam_init),
        grid=(b, nh),
        in_specs=[
            pl.BlockSpec((4, DIFF_HEAD_DIM), const),
            pl.BlockSpec((LANES, 1), const),
            pl.BlockSpec((1, s, LANES), head),
            pl.BlockSpec((1, s, LANES), head),
            pl.BlockSpec((1, s, LANES), head),
        ],
        out_specs=pl.BlockSpec((1, s, LANES), head),
        out_shape=jax.ShapeDtypeStruct((b, s, w), BF16),
        scratch_shapes=[
            pltpu.VMEM((LANES + ATTN_ONES_ROWS, 2 * s), F32),
            pltpu.VMEM((1, 2 * s), F32),
        ],
        compiler_params=_params(("arbitrary", "arbitrary")),
        name="diff_attn",
    )(lam_p, gn_col, dq, dk, dv)


def _seg_scan(x, seg_off, seg_len, op, fill):
    d = 1
    while d < seg_len:
        x = op(x, jnp.where(seg_off >= d, pltpu.roll(x, d, 1), fill))
        d *= 2
    return x


def _log_sigmoid(x):
    return -(jnp.maximum(-x, 0.0) + jnp.log1p(jnp.exp(-jnp.abs(x))))


def _split3(x):
    hi = x.astype(BF16).astype(F32)
    r = x - hi
    mid = r.astype(BF16).astype(F32)
    lo = (r - mid).astype(BF16).astype(F32)
    return hi, mid, lo


GP_W_INTER, GP_FLOOR, GP_W, GP_DECAY, GP_A, GP_NEG_U, GP_COUNT = 0, 1, 2, 3, 4, 7, 10


def _mlstm_gates_kernel(gate_ref, bias_ref, out_ref):
    rows, s = gate_ref.shape
    nseq = rows // SUBLANES
    lc = MLSTM_CHUNK
    nh = N_MLSTM_HEADS
    g = gate_ref[...] + bias_ref[...]
    logf = _log_sigmoid(pltpu.roll(g, rows - nh, 0))
    seg = lax.broadcasted_iota(jnp.int32, (1, s), 1) % lc
    bcum = _seg_scan(logf, seg, lc, lambda x, y: x + y, 0.0)
    a = g - bcum
    cmax = _seg_scan(a, seg, lc, jnp.maximum, -jnp.inf)
    m_prev = jnp.zeros((rows, 1), F32)
    for c in range(s // lc):
        blk = slice(c * lc, (c + 1) * lc)
        a_c, b_c = a[:, blk], bcum[:, blk]
        g_c = b_c[:, lc - 1:lc]
        m_cur = jnp.maximum(g_c + m_prev, g_c + jnp.max(a_c, axis=1, keepdims=True))
        u = jnp.maximum(m_prev, cmax[:, blk])
        planes = [jnp.exp(m_prev - u),
                  jnp.exp(-(u + b_c)),
                  jnp.exp(g_c + a_c - m_cur),
                  jnp.broadcast_to(jnp.exp(g_c + m_prev - m_cur), (rows, lc))]
        planes += list(_split3(a_c * LOG2E)) + list(_split3(-u * LOG2E))
        for k, pv in enumerate(planes):
            for bi in range(nseq):
                out_ref[bi, c, k * SUBLANES:(k + 1) * SUBLANES, :] = (
                    pv[bi * SUBLANES:(bi + 1) * SUBLANES, :])
        m_prev = m_cur


def _mlstm_gates(gates2, bias2, *, batch, seq):
    lc = MLSTM_CHUNK
    gb = MLSTM_GATE_SEQS
    return pl.pallas_call(
        _mlstm_gates_kernel,
        grid=(batch // gb,),
        in_specs=[
            pl.BlockSpec((gb * SUBLANES, seq), lambda i: (i, 0)),
            pl.BlockSpec((gb * SUBLANES, 1), lambda i: (0, 0)),
        ],
        out_specs=pl.BlockSpec((gb, seq // lc, GP_COUNT * SUBLANES, lc),
                               lambda i: (i, 0, 0, 0)),
        out_shape=jax.ShapeDtypeStruct((batch, seq // lc, GP_COUNT * SUBLANES, lc), F32),
        compiler_params=_params(("arbitrary",)),
        name="mlstm_gates",
    )(gates2, bias2)


def _mlstm_kernel(gp_ref, qm_ref, kq_ref, vt_ref, og_ref, gn_ref, out_ref, ct_ref):
    s = qm_ref.shape[1]
    lc = MLSTM_CHUNK
    nh = N_MLSTM_HEADS
    ct_ref[...] = jnp.zeros(ct_ref.shape, F32)
    ones_rows = jnp.ones((MLSTM_ONES_ROWS, lc), BF16)
    causal = (lax.broadcasted_iota(jnp.int32, (lc, lc), 0)
              <= lax.broadcasted_iota(jnp.int32, (lc, lc), 1))
    nt = (((1,), (1,)), ((), ()))
    tn = (((0,), (0,)), ((), ()))
    gnb = [jnp.broadcast_to(gn_ref[hd * LANES:(hd + 1) * LANES, :], (LANES, lc))
           for hd in range(nh)]
    row = lax.broadcasted_iota(jnp.int32, (SUBLANES, lc), 0)
    pick = [jnp.where(row == hd, 1.0, 0.0) for hd in range(nh)]

    def plane(gp, k, n=1):
        return gp[k * SUBLANES:(k + n) * SUBLANES]

    def intra(c, hd):
        gp = gp_ref[0, c]
        sl = slice(hd * LANES, (hd + 1) * LANES)
        qm = qm_ref[0, c * lc:(c + 1) * lc, sl]
        kq = kq_ref[0, c * lc:(c + 1) * lc, sl]
        e = pick[hd]
        lhs = jnp.concatenate([plane(gp, GP_A, 3), e, e, e], axis=0).astype(BF16)
        rhs = jnp.concatenate([e, e, e, plane(gp, GP_NEG_U, 3)], axis=0).astype(BF16)
        arg = lax.dot_general(lhs, rhs, tn, preferred_element_type=F32)
        st = lax.dot_general(kq, qm, nt, preferred_element_type=F32)
        return (st * jnp.exp2(jnp.where(causal, arg, -jnp.inf))).astype(BF16)

    units = [(c, hd) for c in range(s // lc) for hd in range(nh)]
    ahead = [intra(*un) for un in units[:MLSTM_SCORES_AHEAD]]
    for ui, (c, hd) in enumerate(units):
        pt = ahead.pop(0)
        if ui + MLSTM_SCORES_AHEAD < len(units):
            ahead.append(intra(*units[ui + MLSTM_SCORES_AHEAD]))
        start = c * lc
        gp = gp_ref[0, c]
        sl = slice(hd * LANES, (hd + 1) * LANES)
        qm = qm_ref[0, pl.ds(start, lc), sl]
        kq = kq_ref[0, pl.ds(start, lc), sl]
        vt = jnp.concatenate([vt_ref[0, c, sl, :], ones_rows], axis=0)
        w_inter = plane(gp, GP_W_INTER)[hd:hd + 1]
        floor = plane(gp, GP_FLOOR)[hd:hd + 1]
        w_row = plane(gp, GP_W)[hd:hd + 1]
        decay = plane(gp, GP_DECAY)[hd:hd + 1, 0:1]
        ct = ct_ref[hd]
        nd = (jnp.dot(vt, pt, preferred_element_type=F32)
              + w_inter * lax.dot_general(ct.astype(BF16), qm, nt,
                                          preferred_element_type=F32))
        den = jnp.maximum(jnp.abs(nd[LANES:LANES + 1]), floor)
        ht = nd[0:LANES] / den
        ms = jnp.mean(ht * ht, axis=0, keepdims=True)
        hn = ht * lax.rsqrt(ms + EPS) * gnb[hd]
        og = og_ref[0, pl.ds(start, lc), sl].astype(F32)
        out_ref[0, pl.ds(start, lc), sl] = (hn.T * og).astype(out_ref.dtype)
        vw = (vt.astype(F32) * w_row).astype(BF16)
        ct_ref[hd] = decay * ct + jnp.dot(vw, kq, preferred_element_type=F32)


def _mlstm(gplanes, qm, kq, mvt, og, gn_col):
    b, s, w = qm.shape
    lc = MLSTM_CHUNK
    seq = lambda bi: (bi, 0, 0)
    seq4 = lambda bi: (bi, 0, 0, 0)
    return pl.pallas_call(
        _mlstm_kernel,
        grid=(b,),
        in_specs=[
            pl.BlockSpec((1,) + gplanes.shape[1:], seq4),
            pl.BlockSpec((1, s, w), seq),
            pl.BlockSpec((1, s, w), seq),
            pl.BlockSpec((1, s // lc, w, lc), seq4),
            pl.BlockSpec((1, s, w), seq),
            pl.BlockSpec((w, 1), lambda bi: (0, 0)),
        ],
        out_specs=pl.BlockSpec((1, s, w), seq),
        out_shape=jax.ShapeDtypeStruct((b, s, w), BF16),
        scratch_shapes=[
            pltpu.VMEM((N_MLSTM_HEADS, LANES + MLSTM_ONES_ROWS, LANES), F32),
        ],
        compiler_params=_params(("arbitrary",)),
        name="mlstm",
    )(gplanes, qm, kq, mvt, og, gn_col)


def _out_mlp_kernel(x_ref, od_ref, om_ref, wod_ref, wom_ref, gpost_ref, gpre_ref,
                    wu_ref, wdn_ref, gmlp_ref, o_ref):
    half = x_ref.shape[0] // 2
    rows = (slice(0, half), slice(half, 2 * half))

    def mix(rs):
        mixed = (jnp.dot(od_ref[rs, :], wod_ref[...], preferred_element_type=F32)
                 + jnp.dot(om_ref[rs, :], wom_ref[...], preferred_element_type=F32))
        x1 = x_ref[rs, :] + _rms(mixed, gpost_ref[...])
        return x1, _rms(x1, gpre_ref[...]).astype(BF16)

    def mlp(h):
        acc = None
        for f in range(wu_ref.shape[1] // MLP_FF_CHUNK):
            sl = slice(f * MLP_FF_CHUNK, (f + 1) * MLP_FF_CHUNK)
            a = jnp.maximum(jnp.dot(h, wu_ref[:, sl], preferred_element_type=F32), 0.0)
            part = jnp.dot((a * a).astype(BF16), wdn_ref[sl, :], preferred_element_type=F32)
            acc = part if acc is None else acc + part
        return acc

    xa, ha = mix(rows[0])
    xb, hb = mix(rows[1])
    acc_a = mlp(ha)
    acc_b = mlp(hb)
    o_ref[rows[0], :] = xa + _rms(acc_a, gmlp_ref[...])
    o_ref[rows[1], :] = xb + _rms(acc_b, gmlp_ref[...])


def _out_mlp(x2, od, om, wod, wom, gpost, gpre, wu, wdn, gmlp):
    t, d = x2.shape
    tm = MLP_ROWS
    row = lambda i: (i, 0)
    const = lambda i: (0, 0)
    resident = lambda shape: pl.BlockSpec(shape, const, pipeline_mode=pl.Buffered(1))
    return pl.pallas_call(
        _out_mlp_kernel,
        grid=(t // tm,),
        in_specs=[
            pl.BlockSpec((tm, d), row),
            pl.BlockSpec((tm, od.shape[1]), row),
            pl.BlockSpec((tm, om.shape[1]), row),
            resident(wod.shape),
            resident(wom.shape),
            pl.BlockSpec((1, d), const),
            pl.BlockSpec((1, d), const),
            resident(wu.shape),
            resident(wdn.shape),
            pl.BlockSpec((1, d), const),
        ],
        out_specs=pl.BlockSpec((tm, d), row),
        out_shape=jax.ShapeDtypeStruct((t, d), F32),
        compiler_params=_params(("arbitrary",)),
        name="out_mlp",
    )(x2, od, om, wod, wom, gpost, gpre, wu, wdn, gmlp)


def _rope_tables(seq):
    d = DIFF_HEAD_DIM
    inv = ROPE_THETA ** (-jnp.arange(0, d, 2, dtype=F32) / d)
    ang = jnp.arange(seq, dtype=jnp.int32).astype(F32)[:, None] * inv[None, :]
    cos = jnp.cos(ang)
    sin = jnp.sin(ang)
    zero = jnp.zeros_like(sin)
    reps = LANES // d
    cos_t = jnp.tile(jnp.concatenate([cos, cos], axis=1), (1, reps))
    slo_t = jnp.tile(jnp.concatenate([-sin, zero], axis=1), (1, reps))
    shi_t = jnp.tile(jnp.concatenate([zero, sin], axis=1), (1, reps))
    return cos_t, slo_t, shi_t


def _layer(x, l, norm_mix_pre, w_in, conv_w, conv_b, b_igate, b_fgate, lambda_q1,
           lambda_k1, lambda_q2, lambda_k2, diff_norm, mlstm_norm, w_out,
           norm_mix_post, norm_mlp_pre, w_up, w_down, norm_mlp_post):
    b, s, d = x.shape
    nh, dqk = N_MLSTM_HEADS, MLSTM_QK_DIM
    x2 = x.reshape(b * s, d)

    w = w_in[l]
    wq = w[:, 1536:1792].reshape(d, nh, dqk)
    wk = w[:, 1792:2048].reshape(d, nh, dqk)
    wqk = jnp.concatenate([wq, wk], axis=2).reshape(d, 2 * nh * dqk)
    wg = jnp.pad(w[:, 3072:3080], ((0, 0), (0, LANES - 2 * nh)))
    w_main = jnp.concatenate([w[:, 0:1536], wqk, w[:, 2048:3072], wg], axis=1).astype(BF16)

    def qk_interleave(v):
        lead = v.shape[:-1]
        q = v[..., :nh * dqk].reshape(lead + (nh, dqk))
        k = v[..., nh * dqk:].reshape(lead + (nh, dqk))
        return jnp.concatenate([q, k], axis=-1).reshape(lead + (2 * nh * dqk,))

    cw = qk_interleave(conv_w[l])
    cb = qk_interleave(conv_b[l])[None, :]
    qs = qk_interleave(jnp.concatenate([jnp.full((nh * dqk,), dqk ** -0.5, F32),
                                        jnp.zeros((nh * dqk,), F32)]))[None, :]
    cos_t, slo_t, shi_t = _rope_tables(s)

    dq, dk, dv, qm, kq, mvt, og, gates = _in_proj(
        x2, norm_mix_pre[l][None, :], w_main, cos_t, slo_t, shi_t, cw, cb, qs,
        batch=b, seq=s)

    lam_p = jnp.stack([lambda_q1[l], lambda_k1[l], lambda_q2[l], lambda_k2[l]]).astype(F32)
    o_diff = _diff_attn(lam_p, diff_norm[l][:, None], dq.reshape(b, s, -1),
                        dk.reshape(b, s, -1), dv.reshape(b, s, -1),
                        lam_init=_lambda_init(l))

    bias = jnp.concatenate([b_igate[l], b_fgate[l]]).astype(F32)[:, None]
    gplanes = _mlstm_gates(gates.reshape(b * SUBLANES, s),
                           jnp.tile(bias, (MLSTM_GATE_SEQS, 1)), batch=b, seq=s)
    o_mlstm = _mlstm(gplanes, qm.reshape(b, s, -1), kq.reshape(b, s, -1), mvt,
                     og.reshape(b, s, -1), mlstm_norm[l].reshape(-1, 1))

    wo = w_out[l].astype(BF16)
    nd = o_diff.shape[-1]
    out = _out_mlp(x2, o_diff.reshape(b * s, -1), o_mlstm.reshape(b * s, -1),
                   wo[:nd], wo[nd:], norm_mix_post[l][None, :], norm_mlp_pre[l][None, :],
                   w_up[l].astype(BF16), w_down[l].astype(BF16), norm_mlp_post[l][None, :])
    return out.reshape(b, s, d)


def kernel(x, norm_mix_pre, w_in, conv_w, conv_b, b_igate, b_fgate, lambda_q1, lambda_k1,
           lambda_q2, lambda_k2, diff_norm, mlstm_norm, w_out, norm_mix_post,
           norm_mlp_pre, w_up, w_down, norm_mlp_post):
    for l in range(w_in.shape[0]):
        x = _layer(x, l, norm_mix_pre, w_in, conv_w, conv_b, b_igate, b_fgate,
                   lambda_q1, lambda_k1, lambda_q2, lambda_k2, diff_norm, mlstm_norm,
                   w_out, norm_mix_post, norm_mlp_pre, w_up, w_down, norm_mlp_post)
    return x
```

```python
import functools
import math

import numpy as np
import jax
import jax.numpy as jnp
from jax import lax
from jax.experimental import pallas as pl
from jax.experimental.pallas import tpu as pltpu

F32 = jnp.float32
BF16 = jnp.bfloat16

N_DIFF_HEADS = 4
DIFF_HEAD_DIM = 64
N_MLSTM_HEADS = 4
MLSTM_QK_DIM = 64
MLSTM_V_DIM = 128
CONV_WIDTH = 4
ROPE_THETA = 10000.0
EPS = 1e-6

LANES = 128
SUBLANES = 8
VMEM_LIMIT_BYTES = 56 * 1024 * 1024

IN_PROJ_ROWS = 1024
IN_PROJ_AHEAD = 2
LOG2E = math.log2(math.e)
ATTN_ONES_ROWS = 16
ATTN_Q_ROWS = 256
ATTN_KV_ROWS = 512
ATTN_HEADS_PER_STEP = 2
ATTN_SCORES_AHEAD = 4
MLSTM_CHUNK = 256
MLSTM_ONES_ROWS = 16
MLSTM_GATE_SEQS = 8
MLSTM_SCORES_AHEAD = 2
MLP_ROWS = 512
MLP_FF_CHUNK = 1024


def _lambda_init(layer):
    return 0.8 - 0.6 * math.exp(-0.3 * layer)


def _params(semantics, flags=None):
    return pltpu.CompilerParams(dimension_semantics=semantics,
                                vmem_limit_bytes=VMEM_LIMIT_BYTES, flags=flags)


def _rms(x, g):
    return x * lax.rsqrt(jnp.mean(x * x, axis=-1, keepdims=True) + EPS) * g


def _in_proj_kernel(x_ref, g_ref, w_ref, cos_ref, slo_ref, shi_ref, cw_ref, cb_ref,
                    qs_ref, wo_ref, wu_ref, wd_ref, dq_ref, dk_ref, dv_ref, qm_ref, kq_ref,
                    mvt_ref, og_ref, gate_ref, wo16_ref, wu16_ref, wd16_ref, carry_ref,
                    *, tiles_per_seq):
    tm = x_ref.shape[0]
    i = pl.program_id(0)

    wo16_ref[...] = wo_ref[...].astype(BF16)
    wu16_ref[...] = wu_ref[...].astype(BF16)
    wd16_ref[...] = wd_ref[...].astype(BF16)

    @pl.when(i % tiles_per_seq == 0)
    def _():
        carry_ref[...] = jnp.zeros(carry_ref.shape, F32)

    lc = MLSTM_CHUNK
    half = tm // 2
    rows8 = lax.broadcasted_iota(jnp.int32, (SUBLANES, 1), 0)
    cw = cw_ref[...]

    def project(r0, h, tail):
        rs = slice(r0, r0 + half)

        def conv_qk(pre):
            y = cb_ref[...] + cw[3:4, :] * pre
            for j in range(1, CONV_WIDTH):
                sh = pltpu.roll(pre, j, 0)
                head = jnp.where(rows8 < j, pltpu.roll(tail, j, 0), sh[0:SUBLANES, :])
                sh = jnp.concatenate([head, sh[SUBLANES:, :]], axis=0)
                y = y + cw[3 - j:4 - j, :] * sh
            y = y * jax.nn.sigmoid(y)
            qm_ref[rs, :] = (y * qs_ref[...]).astype(BF16)
            kq_ref[rs, :] = jnp.concatenate(
                [pltpu.roll(y[:, gi * LANES:(gi + 1) * LANES], MLSTM_QK_DIM, 1)
                 for gi in range(y.shape[1] // LANES)], axis=1).astype(BF16)

        def values_t(pmv):
            for c in range(half // lc):
                mvt_ref[0, r0 // lc + c] = pmv[c * lc:(c + 1) * lc, :].T.astype(BF16)

        def gates(pg):
            gate_ref[0, :, rs] = pg.T[0:SUBLANES, :]

        def out_gate(po):
            og_ref[rs, :] = jax.nn.sigmoid(po).astype(BF16)

        def rope(p):
            cos, s_lo, s_hi = cos_ref[rs, :], slo_ref[rs, :], shi_ref[rs, :]
            outs = []
            for gi in range(p.shape[1] // LANES):
                v = p[:, gi * LANES:(gi + 1) * LANES]
                outs.append(v * cos + pltpu.roll(v, LANES - 32, 1) * s_lo
                            + pltpu.roll(v, 32, 1) * s_hi)
            return jnp.concatenate(outs, axis=1)

        def attn_q(p):
            dq_ref[rs, :] = (rope(p) * (DIFF_HEAD_DIM ** -0.5 * LOG2E)).astype(BF16)

        def attn_k(p):
            dk_ref[rs, :] = rope(p).astype(BF16)

        def attn_v(p):
            dv_ref[rs, :] = p.astype(BF16)

        segments = [((1536, 2048), conv_qk), ((2048, 2560), values_t), ((3072, 3200), gates),
                    ((2560, 3072), out_gate), ((0, 512), attn_q), ((512, 1024), attn_k),
                    ((1024, 1536), attn_v)]
        proj = lambda cols: jnp.dot(h, w_ref[:, cols[0]:cols[1]], preferred_element_type=F32)
        ahead = [proj(cols) for cols, _ in segments[:IN_PROJ_AHEAD]]
        pre = ahead[0]
        for si, (_, epilogue) in enumerate(segments):
            p = ahead.pop(0)
            if si + IN_PROJ_AHEAD < len(segments):
                ahead.append(proj(segments[si + IN_PROJ_AHEAD][0]))
            epilogue(p)
        return pre[half - SUBLANES:half, :]

    h_a = _rms(x_ref[0:half, :], g_ref[...]).astype(BF16)
    h_b = _rms(x_ref[half:tm, :], g_ref[...]).astype(BF16)
    tail_a = project(0, h_a, carry_ref[...])
    carry_ref[...] = project(half, h_b, tail_a)


def _in_proj(x2, g, w_main, cos_t, slo_t, shi_t, cw, cb, qs, later_w, *, batch, seq):
    t, d = x2.shape
    tm = IN_PROJ_ROWS
    lc = MLSTM_CHUNK
    tps = seq // tm
    steps = t // tm
    nw = w_main.shape[1]
    row = lambda i: (i, 0)
    const = lambda i: (0, 0)
    pos = lambda i: (i % tps, 0)
    out_bf = jax.ShapeDtypeStruct((t, 512), BF16)
    slab_specs = [pl.BlockSpec((wl.shape[0] // steps, wl.shape[1]), row) for wl in later_w]
    slab_shapes = [jax.ShapeDtypeStruct(wl.shape, BF16) for wl in later_w]
    return pl.pallas_call(
        functools.partial(_in_proj_kernel, tiles_per_seq=tps),
        grid=(t // tm,),
        in_specs=[
            pl.BlockSpec((tm, d), row),
            pl.BlockSpec((1, d), const),
            pl.BlockSpec((d, nw), const, pipeline_mode=pl.Buffered(1)),
            pl.BlockSpec((tm, LANES), pos),
            pl.BlockSpec((tm, LANES), pos),
            pl.BlockSpec((tm, LANES), pos),
            pl.BlockSpec((CONV_WIDTH, 512), const),
            pl.BlockSpec((1, 512), const),
            pl.BlockSpec((1, 512), const),
        ] + slab_specs,
        out_specs=[
            pl.BlockSpec((tm, 512), row),
            pl.BlockSpec((tm, 512), row),
            pl.BlockSpec((tm, 512), row),
            pl.BlockSpec((tm, 512), row),
            pl.BlockSpec((tm, 512), row),
            pl.BlockSpec((1, tm // lc, 512, lc), lambda i: (i // tps, i % tps, 0, 0)),
            pl.BlockSpec((tm, 512), row),
            pl.BlockSpec((1, SUBLANES, tm), lambda i: (i // tps, 0, i % tps)),
        ] + slab_specs,
        out_shape=[out_bf, out_bf, out_bf, out_bf, out_bf,
                   jax.ShapeDtypeStruct((batch, seq // lc, 512, lc), BF16), out_bf,
                   jax.ShapeDtypeStruct((batch, SUBLANES, seq), F32)] + slab_shapes,
        scratch_shapes=[pltpu.VMEM((SUBLANES, 512), F32)],
        compiler_params=_params(("arbitrary",)),
        name="in_proj",
    )(x2, g, w_main, cos_t, slo_t, shi_t, cw, cb, qs, *later_w)


def _diff_attn_kernel(lam_ref, gn_ref, q_ref, k_ref, v_ref, o_ref,
                      acc_ref, m_ref, *, lam_init):
    s = q_ref.shape[1]
    tk = ATTN_KV_ROWS
    tf = ATTN_Q_ROWS
    lp = lam_ref[...]
    lam = (jnp.exp(jnp.sum(lp[0:1] * lp[1:2], axis=-1, keepdims=True))
           - jnp.exp(jnp.sum(lp[2:3] * lp[3:4], axis=-1, keepdims=True)) + lam_init)

    lane = lax.broadcasted_iota(jnp.int32, (1, LANES), 1)
    nt = (((1,), (1,)), ((), ()))
    nheads = q_ref.shape[2] // LANES
    hsl = [slice(hh * LANES, (hh + 1) * LANES) for hh in range(nheads)]
    qmaps = []
    for hh in range(nheads):
        q = q_ref[0, :, hsl[hh]]
        zero = jnp.zeros_like(q)
        qmaps.append((jnp.where(lane < DIFF_HEAD_DIM, q, zero),
                      jnp.where(lane >= DIFF_HEAD_DIM, q, zero)))

    hk = tk // 2
    units = []
    for lo in range(0, s, tk):
        for k0, k1, c0, c1 in ((lo, lo + hk, lo, lo + hk), (lo, lo + tk, lo + hk, s)):
            units += [(hh, k0, k1, c0, c1) for hh in range(nheads)]

    def scores(unit):
        hh, k0, k1, c0, c1 = unit
        n = c1 - c0
        qq = jnp.concatenate([qmaps[hh][0][c0:c1, :], qmaps[hh][1][c0:c1, :]], axis=0)
        st = lax.dot_general(k_ref[0, k0:k1, hsl[hh]], qq, nt,
                             preferred_element_type=F32)
        nmask = k1 - c0
        keep = (lax.broadcasted_iota(jnp.int32, (k1 - k0, nmask), 1) + (c0 - k0)
                >= lax.broadcasted_iota(jnp.int32, (k1 - k0, nmask), 0))
        parts = []
        for base in (0, n):
            parts.append(jnp.where(keep, st[:, base:base + nmask], -jnp.inf))
            if n > nmask:
                parts.append(st[:, base + nmask:base + n])
        return jnp.concatenate(parts, axis=1)

    def get(ref, hh, c0, c1):
        return jnp.concatenate([ref[hh, :, c0:c1], ref[hh, :, s + c0:s + c1]], axis=1)

    def put(ref, hh, c0, c1, val):
        n = c1 - c0
        ref[hh, :, c0:c1] = val[:, :n]
        ref[hh, :, s + c0:s + c1] = val[:, n:]

    vt_tiles = {}

    def vt_tile(hh, r):
        if (hh, r) not in vt_tiles:
            vt_tiles[(hh, r)] = v_ref[0, r:r + LANES, hsl[hh]].astype(F32).T.astype(BF16)
        return vt_tiles[(hh, r)]

    def vt_aug(hh, k0, k1):
        return jnp.concatenate(
            [jnp.concatenate([vt_tile(hh, r) for r in range(k0, k1, LANES)], axis=1),
             jnp.ones((ATTN_ONES_ROWS, k1 - k0), BF16)], axis=0)

    ahead = [scores(un) for un in units[:ATTN_SCORES_AHEAD]]
    for u, (hh, k0, k1, c0, c1) in enumerate(units):
        st = ahead.pop(0)
        if u + ATTN_SCORES_AHEAD < len(units):
            ahead.append(scores(units[u + ATTN_SCORES_AHEAD]))
        m_blk = jnp.max(st, axis=0, keepdims=True)
        if k0 == 0:
            put(m_ref, hh, c0, c1, m_blk)
            put(acc_ref, hh, c0, c1,
                jnp.dot(vt_aug(hh, k0, k1), jnp.exp2(st - m_blk).astype(BF16),
                        preferred_element_type=F32))
            continue
        m_old = get(m_ref, hh, c0, c1)
        m_new = jnp.maximum(m_old, m_blk)
        alpha = jnp.exp2(m_old - m_new)
        p = jnp.exp2(st - m_new).astype(BF16)
        put(m_ref, hh, c0, c1, m_new)
        put(acc_ref, hh, c0, c1, alpha * get(acc_ref, hh, c0, c1) + jnp.dot(
            vt_aug(hh, k0, k1), p, preferred_element_type=F32))

    for hh in range(nheads):
        for c0 in range(0, s, tf):
            a1 = acc_ref[hh, 0:LANES, c0:c0 + tf] / acc_ref[hh, LANES:LANES + 1, c0:c0 + tf]
            a2 = (acc_ref[hh, 0:LANES, s + c0:s + c0 + tf]
                  / acc_ref[hh, LANES:LANES + 1, s + c0:s + c0 + tf])
            ot = a1 - lam * a2
            ms = jnp.mean(ot * ot, axis=0, keepdims=True)
            on = ot * lax.rsqrt(ms + EPS) * gn_ref[...] * (1.0 - lam_init)
            o_ref[0, c0:c0 + tf, hsl[hh]] = on.T.astype(o_ref.dtype)


def _diff_attn(lam_p, gn_col, dq, dk, dv, *, lam_init):
    b, s, w = dq.shape
    hp = ATTN_HEADS_PER_STEP
    hw = hp * LANES
    head = lambda bi, hi: (bi, 0, hi)
    const = lambda bi, hi: (0, 0)
    return pl.pallas_call(
        functools.partial(_diff_attn_kernel, lam_init=lam_init),
        grid=(b, w // hw),
        in_specs=[
            pl.BlockSpec((4, DIFF_HEAD_DIM), const),
            pl.BlockSpec((LANES, 1), const),
            pl.BlockSpec((1, s, hw), head),
            pl.BlockSpec((1, s, hw), head),
            pl.BlockSpec((1, s, hw), head),
        ],
        out_specs=pl.BlockSpec((1, s, hw), head),
        out_shape=jax.ShapeDtypeStruct((b, s, w), BF16),
        scratch_shapes=[
            pltpu.VMEM((hp, LANES + ATTN_ONES_ROWS, 2 * s), F32),
            pltpu.VMEM((hp, 1, 2 * s), F32),
        ],
        compiler_params=_params(("arbitrary", "arbitrary")),
        name="diff_attn",
    )(lam_p, gn_col, dq, dk, dv)


def _seg_scan(x, seg_off, seg_len, op, fill):
    d = 1
    while d < seg_len:
        x = op(x, jnp.where(seg_off >= d, pltpu.roll(x, d, 1), fill))
        d *= 2
    return x


def _log_sigmoid(x):
    return -(jnp.maximum(-x, 0.0) + jnp.log1p(jnp.exp(-jnp.abs(x))))


def _split3(x):
    hi = x.astype(BF16).astype(F32)
    r = x - hi
    mid = r.astype(BF16).astype(F32)
    lo = (r - mid).astype(BF16).astype(F32)
    return hi, mid, lo


GP_W_INTER, GP_FLOOR, GP_W, GP_DECAY, GP_A, GP_NEG_U, GP_COUNT = 0, 1, 2, 3, 4, 7, 10


def _mlstm_gates_kernel(gate_ref, bias_ref, out_ref):
    rows, s = gate_ref.shape
    nseq = rows // SUBLANES
    lc = MLSTM_CHUNK
    nh = N_MLSTM_HEADS
    g = gate_ref[...] + bias_ref[...]
    logf = _log_sigmoid(pltpu.roll(g, rows - nh, 0))
    seg = lax.broadcasted_iota(jnp.int32, (1, s), 1) % lc
    bcum = _seg_scan(logf, seg, lc, lambda x, y: x + y, 0.0)
    a = g - bcum
    cmax = _seg_scan(a, seg, lc, jnp.maximum, -jnp.inf)
    m_prev = jnp.zeros((rows, 1), F32)
    for c in range(s // lc):
        blk = slice(c * lc, (c + 1) * lc)
        a_c, b_c = a[:, blk], bcum[:, blk]
        g_c = b_c[:, lc - 1:lc]
        m_cur = jnp.maximum(g_c + m_prev, g_c + jnp.max(a_c, axis=1, keepdims=True))
        u = jnp.maximum(m_prev, cmax[:, blk])
        planes = [jnp.exp(m_prev - u),
                  jnp.exp(-(u + b_c)),
                  jnp.exp(g_c + a_c - m_cur),
                  jnp.broadcast_to(jnp.exp(g_c + m_prev - m_cur), (rows, lc))]
        planes += list(_split3(a_c * LOG2E)) + list(_split3(-u * LOG2E))
        for k, pv in enumerate(planes):
            for bi in range(nseq):
                out_ref[bi, c, k * SUBLANES:(k + 1) * SUBLANES, :] = (
                    pv[bi * SUBLANES:(bi + 1) * SUBLANES, :])
        m_prev = m_cur


def _mlstm_gates(gates2, bias2, *, batch, seq):
    lc = MLSTM_CHUNK
    gb = MLSTM_GATE_SEQS
    return pl.pallas_call(
        _mlstm_gates_kernel,
        grid=(batch // gb,),
        in_specs=[
            pl.BlockSpec((gb * SUBLANES, seq), lambda i: (i, 0)),
            pl.BlockSpec((gb * SUBLANES, 1), lambda i: (0, 0)),
        ],
        out_specs=pl.BlockSpec((gb, seq // lc, GP_COUNT * SUBLANES, lc),
                               lambda i: (i, 0, 0, 0)),
        out_shape=jax.ShapeDtypeStruct((batch, seq // lc, GP_COUNT * SUBLANES, lc), F32),
        compiler_params=_params(("arbitrary",)),
        name="mlstm_gates",
    )(gates2, bias2)


def _mlstm_kernel(gp_ref, qm_ref, kq_ref, vt_ref, og_ref, gn_ref, out_ref, ct_ref):
    s = qm_ref.shape[1]
    lc = MLSTM_CHUNK
    nh = N_MLSTM_HEADS
    ct_ref[...] = jnp.zeros(ct_ref.shape, F32)
    ones_rows = jnp.ones((MLSTM_ONES_ROWS, lc), BF16)
    causal = (lax.broadcasted_iota(jnp.int32, (lc, lc), 0)
              <= lax.broadcasted_iota(jnp.int32, (lc, lc), 1))
    nt = (((1,), (1,)), ((), ()))
    tn = (((0,), (0,)), ((), ()))
    gnb = [jnp.broadcast_to(gn_ref[hd * LANES:(hd + 1) * LANES, :], (LANES, lc))
           for hd in range(nh)]
    row = lax.broadcasted_iota(jnp.int32, (SUBLANES, lc), 0)
    pick = [jnp.where(row == hd, 1.0, 0.0) for hd in range(nh)]

    def plane(gp, k, n=1):
        return gp[k * SUBLANES:(k + n) * SUBLANES]

    def intra(c, hd):
        gp = gp_ref[0, c]
        sl = slice(hd * LANES, (hd + 1) * LANES)
        qm = qm_ref[0, c * lc:(c + 1) * lc, sl]
        kq = kq_ref[0, c * lc:(c + 1) * lc, sl]
        e = pick[hd]
        lhs = jnp.concatenate([plane(gp, GP_A, 3), e, e, e], axis=0).astype(BF16)
        rhs = jnp.concatenate([e, e, e, plane(gp, GP_NEG_U, 3)], axis=0).astype(BF16)
        arg = lax.dot_general(lhs, rhs, tn, preferred_element_type=F32)
        st = lax.dot_general(kq, qm, nt, preferred_element_type=F32)
        return (st * jnp.exp2(jnp.where(causal, arg, -jnp.inf))).astype(BF16)

    units = [(c, hd) for c in range(s // lc) for hd in range(nh)]
    ahead = [intra(*un) for un in units[:MLSTM_SCORES_AHEAD]]
    for ui, (c, hd) in enumerate(units):
        pt = ahead.pop(0)
        if ui + MLSTM_SCORES_AHEAD < len(units):
            ahead.append(intra(*units[ui + MLSTM_SCORES_AHEAD]))
        start = c * lc
        gp = gp_ref[0, c]
        sl = slice(hd * LANES, (hd + 1) * LANES)
        qm = qm_ref[0, pl.ds(start, lc), sl]
        kq = kq_ref[0, pl.ds(start, lc), sl]
        vt = jnp.concatenate([vt_ref[0, c, sl, :], ones_rows], axis=0)
        w_inter = plane(gp, GP_W_INTER)[hd:hd + 1]
        floor = plane(gp, GP_FLOOR)[hd:hd + 1]
        w_row = plane(gp, GP_W)[hd:hd + 1]
        decay = plane(gp, GP_DECAY)[hd:hd + 1, 0:1]
        ct = ct_ref[hd]
        nd = (jnp.dot(vt, pt, preferred_element_type=F32)
              + w_inter * lax.dot_general(ct.astype(BF16), qm, nt,
                                          preferred_element_type=F32))
        den = jnp.maximum(jnp.abs(nd[LANES:LANES + 1]), floor)
        ht = nd[0:LANES] / den
        ms = jnp.mean(ht * ht, axis=0, keepdims=True)
        hn = ht * lax.rsqrt(ms + EPS) * gnb[hd]
        og = og_ref[0, pl.ds(start, lc), sl].astype(F32)
        out_ref[0, pl.ds(start, lc), sl] = (hn.T * og).astype(out_ref.dtype)
        vw = (vt.astype(F32) * w_row).astype(BF16)
        ct_ref[hd] = decay * ct + jnp.dot(vw, kq, preferred_element_type=F32)


def _mlstm(gplanes, qm, kq, mvt, og, gn_col):
    b, s, w = qm.shape
    lc = MLSTM_CHUNK
    seq = lambda bi: (bi, 0, 0)
    seq4 = lambda bi: (bi, 0, 0, 0)
    return pl.pallas_call(
        _mlstm_kernel,
        grid=(b,),
        in_specs=[
            pl.BlockSpec((1,) + gplanes.shape[1:], seq4),
            pl.BlockSpec((1, s, w), seq),
            pl.BlockSpec((1, s, w), seq),
            pl.BlockSpec((1, s // lc, w, lc), seq4),
            pl.BlockSpec((1, s, w), seq),
            pl.BlockSpec((w, 1), lambda bi: (0, 0)),
        ],
        out_specs=pl.BlockSpec((1, s, w), seq),
        out_shape=jax.ShapeDtypeStruct((b, s, w), BF16),
        scratch_shapes=[
            pltpu.VMEM((N_MLSTM_HEADS, LANES + MLSTM_ONES_ROWS, LANES), F32),
        ],
        compiler_params=_params(("arbitrary",)),
        name="mlstm",
    )(gplanes, qm, kq, mvt, og, gn_col)


def _out_mlp_kernel(x_ref, od_ref, om_ref, wo_ref, gpost_ref, gpre_ref,
                    wu_ref, wdn_ref, gmlp_ref, o_ref):
    half = x_ref.shape[0] // 2
    rows = (slice(0, half), slice(half, 2 * half))
    nd = od_ref.shape[1]

    def mix(rs):
        mixed = (jnp.dot(od_ref[rs, :], wo_ref[0:nd, :], preferred_element_type=F32)
                 + jnp.dot(om_ref[rs, :], wo_ref[nd:, :], preferred_element_type=F32))
        x1 = x_ref[rs, :] + _rms(mixed, gpost_ref[...])
        return x1, _rms(x1, gpre_ref[...]).astype(BF16)

    def mlp(h):
        acc = None
        for f in range(wu_ref.shape[1] // MLP_FF_CHUNK):
            sl = slice(f * MLP_FF_CHUNK, (f + 1) * MLP_FF_CHUNK)
            a = jnp.maximum(jnp.dot(h, wu_ref[:, sl], preferred_element_type=F32), 0.0)
            part = jnp.dot((a * a).astype(BF16), wdn_ref[sl, :], preferred_element_type=F32)
            acc = part if acc is None else acc + part
        return acc

    xa, ha = mix(rows[0])
    xb, hb = mix(rows[1])
    acc_a = mlp(ha)
    acc_b = mlp(hb)
    o_ref[rows[0], :] = xa + _rms(acc_a, gmlp_ref[...])
    o_ref[rows[1], :] = xb + _rms(acc_b, gmlp_ref[...])


def _out_mlp(x2, od, om, wo, gpost, gpre, wu, wdn, gmlp):
    t, d = x2.shape
    tm = MLP_ROWS
    row = lambda i: (i, 0)
    const = lambda i: (0, 0)
    resident = lambda shape: pl.BlockSpec(shape, const, pipeline_mode=pl.Buffered(1))
    return pl.pallas_call(
        _out_mlp_kernel,
        grid=(t // tm,),
        in_specs=[
            pl.BlockSpec((tm, d), row),
            pl.BlockSpec((tm, od.shape[1]), row),
            pl.BlockSpec((tm, om.shape[1]), row),
            resident(wo.shape),
            pl.BlockSpec((1, d), const),
            pl.BlockSpec((1, d), const),
            resident(wu.shape),
            resident(wdn.shape),
            pl.BlockSpec((1, d), const),
        ],
        out_specs=pl.BlockSpec((tm, d), row),
        out_shape=jax.ShapeDtypeStruct((t, d), F32),
        compiler_params=_params(("arbitrary",)),
        name="out_mlp",
    )(x2, od, om, wo, gpost, gpre, wu, wdn, gmlp)


def _rope_tables(seq):
    d = DIFF_HEAD_DIM
    inv = ROPE_THETA ** (-jnp.arange(0, d, 2, dtype=F32) / d)
    ang = jnp.arange(seq, dtype=jnp.int32).astype(F32)[:, None] * inv[None, :]
    cos = jnp.cos(ang)
    sin = jnp.sin(ang)
    zero = jnp.zeros_like(sin)
    reps = LANES // d
    cos_t = jnp.tile(jnp.concatenate([cos, cos], axis=1), (1, reps))
    slo_t = jnp.tile(jnp.concatenate([-sin, zero], axis=1), (1, reps))
    shi_t = jnp.tile(jnp.concatenate([zero, sin], axis=1), (1, reps))
    return cos_t, slo_t, shi_t


def _layer(x, l, norm_mix_pre, w_in, conv_w, conv_b, b_igate, b_fgate, lambda_q1,
           lambda_k1, lambda_q2, lambda_k2, diff_norm, mlstm_norm, w_out,
           norm_mix_post, norm_mlp_pre, w_up, w_down, norm_mlp_post):
    b, s, d = x.shape
    nh, dqk = N_MLSTM_HEADS, MLSTM_QK_DIM
    x2 = x.reshape(b * s, d)

    w = w_in[l]
    wq = w[:, 1536:1792].reshape(d, nh, dqk)
    wk = w[:, 1792:2048].reshape(d, nh, dqk)
    wqk = jnp.concatenate([wq, wk], axis=2).reshape(d, 2 * nh * dqk)
    wg = jnp.pad(w[:, 3072:3080], ((0, 0), (0, LANES - 2 * nh)))
    w_main = jnp.concatenate([w[:, 0:1536], wqk, w[:, 2048:3072], wg], axis=1).astype(BF16)

    def qk_interleave(v):
        lead = v.shape[:-1]
        q = v[..., :nh * dqk].reshape(lead + (nh, dqk))
        k = v[..., nh * dqk:].reshape(lead + (nh, dqk))
        return jnp.concatenate([q, k], axis=-1).reshape(lead + (2 * nh * dqk,))

    cw = qk_interleave(conv_w[l])
    cb = qk_interleave(conv_b[l])[None, :]
    qs = qk_interleave(jnp.concatenate([jnp.full((nh * dqk,), dqk ** -0.5, F32),
                                        jnp.zeros((nh * dqk,), F32)]))[None, :]
    cos_t, slo_t, shi_t = _rope_tables(s)

    dq, dk, dv, qm, kq, mvt, og, gates, wo, wu, wdn = _in_proj(
        x2, norm_mix_pre[l][None, :], w_main, cos_t, slo_t, shi_t, cw, cb, qs,
        (w_out[l], w_up[l], w_down[l]), batch=b, seq=s)

    lam_p = jnp.stack([lambda_q1[l], lambda_k1[l], lambda_q2[l], lambda_k2[l]]).astype(F32)
    o_diff = _diff_attn(lam_p, diff_norm[l][:, None], dq.reshape(b, s, -1),
                        dk.reshape(b, s, -1), dv.reshape(b, s, -1),
                        lam_init=_lambda_init(l))

    bias = jnp.concatenate([b_igate[l], b_fgate[l]]).astype(F32)[:, None]
    gplanes = _mlstm_gates(gates.reshape(b * SUBLANES, s),
                           jnp.tile(bias, (MLSTM_GATE_SEQS, 1)), batch=b, seq=s)
    o_mlstm = _mlstm(gplanes, qm.reshape(b, s, -1), kq.reshape(b, s, -1), mvt,
                     og.reshape(b, s, -1), mlstm_norm[l].reshape(-1, 1))

    out = _out_mlp(x2, o_diff.reshape(b * s, -1), o_mlstm.reshape(b * s, -1),
                   wo, norm_mix_post[l][None, :], norm_mlp_pre[l][None, :],
                   wu, wdn, norm_mlp_post[l][None, :])
    return out.reshape(b, s, d)


def kernel(x, norm_mix_pre, w_in, conv_w, conv_b, b_igate, b_fgate, lambda_q1, lambda_k1,
           lambda_q2, lambda_k2, diff_norm, mlstm_norm, w_out, norm_mix_post,
           norm_mlp_pre, w_up, w_down, norm_mlp_post):
    for l in range(w_in.shape[0]):
        x = _layer(x, l, norm_mix_pre, w_in, conv_w, conv_b, b_igate, b_fgate,
                   lambda_q1, lambda_k1, lambda_q2, lambda_k2, diff_norm, mlstm_norm,
                   w_out, norm_mix_post, norm_mlp_pre, w_up, w_down, norm_mlp_post)
    return x
```

```python
import functools
import math

import jax
import jax.numpy as jnp
from jax import lax
from jax.experimental import pallas as pl
from jax.experimental.pallas import tpu as pltpu

F32 = jnp.float32
BF16 = jnp.bfloat16

DIFF_HEAD_DIM = 64
N_MLSTM_HEADS = 4
MLSTM_QK_DIM = 64
CONV_WIDTH = 4
ROPE_THETA = 10000.0
EPS = 1e-6

SEG = 512
COLS_DQ, COLS_DK, COLS_DV, COLS_MQK, COLS_MV, COLS_MO = (
    (k * SEG, (k + 1) * SEG) for k in range(6))
COLS_GATES = (6 * SEG, 6 * SEG + 128)

LANES = 128
SUBLANES = 8
VMEM_LIMIT_BYTES = 56 * 1024 * 1024

IN_PROJ_ROWS = 1024
LOG2E = math.log2(math.e)
ATTN_ONES_ROWS = 16
ATTN_Q_ROWS = 256
ATTN_KV_ROWS = 512
ATTN_HEADS_PER_STEP = 2
ATTN_SCORES_AHEAD = 4
MLSTM_CHUNK = 256
MLSTM_ONES_ROWS = 16
MLSTM_GATE_SEQS = 8
MLSTM_SCORES_AHEAD = 2
MLP_ROWS = 512
MLP_FF_CHUNK = 1024


def _lambda_init(layer):
    return 0.8 - 0.6 * math.exp(-0.3 * layer)


def _params(semantics):
    return pltpu.CompilerParams(dimension_semantics=semantics,
                                vmem_limit_bytes=VMEM_LIMIT_BYTES)


def _rms(x, g):
    return x * lax.rsqrt(jnp.mean(x * x, axis=-1, keepdims=True) + EPS) * g


def _in_proj_kernel(x_ref, g_ref, w_ref, cos_ref, slo_ref, shi_ref, cw_ref, cb_ref,
                    qs_ref, wo_ref, wu_ref, wd_ref, dq_ref, dk_ref, dv_ref, qm_ref, kq_ref,
                    mvt_ref, og_ref, gate_ref, wo16_ref, wu16_ref, wd16_ref, carry_ref,
                    *, tiles_per_seq):
    tm = x_ref.shape[0]
    i = pl.program_id(0)

    wo16_ref[...] = wo_ref[...].astype(BF16)
    wu16_ref[...] = wu_ref[...].astype(BF16)
    wd16_ref[...] = wd_ref[...].astype(BF16)

    @pl.when(i % tiles_per_seq == 0)
    def _():
        carry_ref[...] = jnp.zeros(carry_ref.shape, F32)

    lc = MLSTM_CHUNK
    half = tm // 2
    rows8 = lax.broadcasted_iota(jnp.int32, (SUBLANES, 1), 0)
    cw = cw_ref[...]

    def project(r0, h, tail):
        rs = slice(r0, r0 + half)

        def conv_qk(pre):
            y = cb_ref[...] + cw[3:4, :] * pre
            for j in range(1, CONV_WIDTH):
                sh = pltpu.roll(pre, j, 0)
                head = jnp.where(rows8 < j, pltpu.roll(tail, j, 0), sh[0:SUBLANES, :])
                sh = jnp.concatenate([head, sh[SUBLANES:, :]], axis=0)
                y = y + cw[3 - j:4 - j, :] * sh
            y = y * jax.nn.sigmoid(y)
            qm_ref[rs, :] = (y * qs_ref[...]).astype(BF16)
            kq_ref[rs, :] = jnp.concatenate(
                [pltpu.roll(y[:, gi * LANES:(gi + 1) * LANES], MLSTM_QK_DIM, 1)
                 for gi in range(y.shape[1] // LANES)], axis=1).astype(BF16)

        def values_t(pmv):
            for c in range(half // lc):
                mvt_ref[0, r0 // lc + c] = pmv[c * lc:(c + 1) * lc, :].T.astype(BF16)

        def gates(pg):
            gate_ref[0, :, rs] = pg.T[0:SUBLANES, :]

        def out_gate(po):
            og_ref[rs, :] = jax.nn.sigmoid(po).astype(BF16)

        def rope(p):
            cos, s_lo, s_hi = cos_ref[rs, :], slo_ref[rs, :], shi_ref[rs, :]
            outs = []
            for gi in range(p.shape[1] // LANES):
                v = p[:, gi * LANES:(gi + 1) * LANES]
                outs.append(v * cos + pltpu.roll(v, LANES - 32, 1) * s_lo
                            + pltpu.roll(v, 32, 1) * s_hi)
            return jnp.concatenate(outs, axis=1)

        def attn_q(p):
            dq_ref[rs, :] = (rope(p) * (DIFF_HEAD_DIM ** -0.5 * LOG2E)).astype(BF16)

        def attn_k(p):
            dk_ref[rs, :] = rope(p).astype(BF16)

        def attn_v(p):
            dv_ref[rs, :] = p.astype(BF16)

        segments = [(COLS_MQK, conv_qk), (COLS_MV, values_t), (COLS_GATES, gates),
                    (COLS_MO, out_gate), (COLS_DQ, attn_q), (COLS_DK, attn_k),
                    (COLS_DV, attn_v)]
        pre = None
        for cols, epilogue in segments:
            p = jnp.dot(h, w_ref[:, cols[0]:cols[1]], preferred_element_type=F32)
            if epilogue is conv_qk:
                pre = p
            epilogue(p)
        return pre[half - SUBLANES:half, :]

    h_a = _rms(x_ref[0:half, :], g_ref[...]).astype(BF16)
    h_b = _rms(x_ref[half:tm, :], g_ref[...]).astype(BF16)
    tail_a = project(0, h_a, carry_ref[...])
    carry_ref[...] = project(half, h_b, tail_a)


def _in_proj(x2, g, w_main, cos_t, slo_t, shi_t, cw, cb, qs, later_w, *, batch, seq):
    t, d = x2.shape
    tm = IN_PROJ_ROWS
    lc = MLSTM_CHUNK
    tps = seq // tm
    steps = t // tm
    nw = w_main.shape[1]
    row = lambda i: (i, 0)
    const = lambda i: (0, 0)
    pos = lambda i: (i % tps, 0)
    out_bf = jax.ShapeDtypeStruct((t, SEG), BF16)
    seg_rows = pl.BlockSpec((tm, SEG), row)
    slab_specs = [pl.BlockSpec((wl.shape[0] // steps, wl.shape[1]), row) for wl in later_w]
    slab_shapes = [jax.ShapeDtypeStruct(wl.shape, BF16) for wl in later_w]
    return pl.pallas_call(
        functools.partial(_in_proj_kernel, tiles_per_seq=tps),
        grid=(t // tm,),
        in_specs=[
            pl.BlockSpec((tm, d), row),
            pl.BlockSpec((1, d), const),
            pl.BlockSpec((d, nw), const, pipeline_mode=pl.Buffered(1)),
            pl.BlockSpec((tm, LANES), pos),
            pl.BlockSpec((tm, LANES), pos),
            pl.BlockSpec((tm, LANES), pos),
            pl.BlockSpec((CONV_WIDTH, SEG), const),
            pl.BlockSpec((1, SEG), const),
            pl.BlockSpec((1, SEG), const),
        ] + slab_specs,
        out_specs=[
            seg_rows, seg_rows, seg_rows, seg_rows, seg_rows,
            pl.BlockSpec((1, tm // lc, SEG, lc), lambda i: (i // tps, i % tps, 0, 0)),
            seg_rows,
            pl.BlockSpec((1, SUBLANES, tm), lambda i: (i // tps, 0, i % tps)),
        ] + slab_specs,
        out_shape=[out_bf, out_bf, out_bf, out_bf, out_bf,
                   jax.ShapeDtypeStruct((batch, seq // lc, SEG, lc), BF16), out_bf,
                   jax.ShapeDtypeStruct((batch, SUBLANES, seq), F32)] + slab_shapes,
        scratch_shapes=[pltpu.VMEM((SUBLANES, SEG), F32)],
        compiler_params=_params(("arbitrary",)),
        name="in_proj",
    )(x2, g, w_main, cos_t, slo_t, shi_t, cw, cb, qs, *later_w)


def _diff_attn_kernel(lam_ref, gn_ref, q_ref, k_ref, v_ref, o_ref,
                      acc_ref, m_ref, *, lam_init):
    s = q_ref.shape[1]
    tk = ATTN_KV_ROWS
    tf = ATTN_Q_ROWS
    lp = lam_ref[...]
    lam = (jnp.exp(jnp.sum(lp[0:1] * lp[1:2], axis=-1, keepdims=True))
           - jnp.exp(jnp.sum(lp[2:3] * lp[3:4], axis=-1, keepdims=True)) + lam_init)

    lane = lax.broadcasted_iota(jnp.int32, (1, LANES), 1)
    nt = (((1,), (1,)), ((), ()))
    nheads = q_ref.shape[2] // LANES
    hsl = [slice(hh * LANES, (hh + 1) * LANES) for hh in range(nheads)]
    qmaps = []
    for hh in range(nheads):
        q = q_ref[0, :, hsl[hh]]
        zero = jnp.zeros_like(q)
        qmaps.append((jnp.where(lane < DIFF_HEAD_DIM, q, zero),
                      jnp.where(lane >= DIFF_HEAD_DIM, q, zero)))

    hk = tk // 2
    units = []
    for lo in range(0, s, tk):
        for k0, k1, c0, c1 in ((lo, lo + hk, lo, lo + hk), (lo, lo + tk, lo + hk, s)):
            units += [(hh, k0, k1, c0, c1) for hh in range(nheads)]

    def scores(unit):
        hh, k0, k1, c0, c1 = unit
        n = c1 - c0
        qq = jnp.concatenate([qmaps[hh][0][c0:c1, :], qmaps[hh][1][c0:c1, :]], axis=0)
        st = lax.dot_general(k_ref[0, k0:k1, hsl[hh]], qq, nt,
                             preferred_element_type=F32)
        nmask = k1 - c0
        keep = (lax.broadcasted_iota(jnp.int32, (k1 - k0, nmask), 1) + (c0 - k0)
                >= lax.broadcasted_iota(jnp.int32, (k1 - k0, nmask), 0))
        parts = []
        for base in (0, n):
            parts.append(jnp.where(keep, st[:, base:base + nmask], -jnp.inf))
            if n > nmask:
                parts.append(st[:, base + nmask:base + n])
        return jnp.concatenate(parts, axis=1)

    def get(ref, hh, c0, c1):
        return jnp.concatenate([ref[hh, :, c0:c1], ref[hh, :, s + c0:s + c1]], axis=1)

    def put(ref, hh, c0, c1, val):
        n = c1 - c0
        ref[hh, :, c0:c1] = val[:, :n]
        ref[hh, :, s + c0:s + c1] = val[:, n:]

    vt_tiles = {}

    def vt_tile(hh, r):
        if (hh, r) not in vt_tiles:
            vt_tiles[(hh, r)] = v_ref[0, r:r + LANES, hsl[hh]].astype(F32).T.astype(BF16)
        return vt_tiles[(hh, r)]

    def vt_aug(hh, k0, k1):
        return jnp.concatenate(
            [jnp.concatenate([vt_tile(hh, r) for r in range(k0, k1, LANES)], axis=1),
             jnp.ones((ATTN_ONES_ROWS, k1 - k0), BF16)], axis=0)

    ahead = [scores(un) for un in units[:ATTN_SCORES_AHEAD]]
    for u, (hh, k0, k1, c0, c1) in enumerate(units):
        st = ahead.pop(0)
        if u + ATTN_SCORES_AHEAD < len(units):
            ahead.append(scores(units[u + ATTN_SCORES_AHEAD]))
        m_blk = jnp.max(st, axis=0, keepdims=True)
        if k0 == 0:
            put(m_ref, hh, c0, c1, m_blk)
            put(acc_ref, hh, c0, c1,
                jnp.dot(vt_aug(hh, k0, k1), jnp.exp2(st - m_blk).astype(BF16),
                        preferred_element_type=F32))
            continue
        m_old = get(m_ref, hh, c0, c1)
        m_new = jnp.maximum(m_old, m_blk)
        alpha = jnp.exp2(m_old - m_new)
        p = jnp.exp2(st - m_new).astype(BF16)
        put(m_ref, hh, c0, c1, m_new)
        put(acc_ref, hh, c0, c1, alpha * get(acc_ref, hh, c0, c1) + jnp.dot(
            vt_aug(hh, k0, k1), p, preferred_element_type=F32))

    for hh in range(nheads):
        for c0 in range(0, s, tf):
            a1 = acc_ref[hh, 0:LANES, c0:c0 + tf] / acc_ref[hh, LANES:LANES + 1, c0:c0 + tf]
            a2 = (acc_ref[hh, 0:LANES, s + c0:s + c0 + tf]
                  / acc_ref[hh, LANES:LANES + 1, s + c0:s + c0 + tf])
            ot = a1 - lam * a2
            ms = jnp.mean(ot * ot, axis=0, keepdims=True)
            on = ot * lax.rsqrt(ms + EPS) * gn_ref[...] * (1.0 - lam_init)
            o_ref[0, c0:c0 + tf, hsl[hh]] = on.T.astype(o_ref.dtype)


def _diff_attn(lam_p, gn_col, dq, dk, dv, *, lam_init):
    b, s, w = dq.shape
    hp = ATTN_HEADS_PER_STEP
    hw = hp * LANES
    head = lambda bi, hi: (bi, 0, hi)
    const = lambda bi, hi: (0, 0)
    return pl.pallas_call(
        functools.partial(_diff_attn_kernel, lam_init=lam_init),
        grid=(b, w // hw),
        in_specs=[
            pl.BlockSpec((4, DIFF_HEAD_DIM), const),
            pl.BlockSpec((LANES, 1), const),
            pl.BlockSpec((1, s, hw), head),
            pl.BlockSpec((1, s, hw), head),
            pl.BlockSpec((1, s, hw), head),
        ],
        out_specs=pl.BlockSpec((1, s, hw), head),
        out_shape=jax.ShapeDtypeStruct((b, s, w), BF16),
        scratch_shapes=[
            pltpu.VMEM((hp, LANES + ATTN_ONES_ROWS, 2 * s), F32),
            pltpu.VMEM((hp, 1, 2 * s), F32),
        ],
        compiler_params=_params(("arbitrary", "arbitrary")),
        name="diff_attn",
    )(lam_p, gn_col, dq, dk, dv)


def _seg_scan(x, seg_off, seg_len, op, fill):
    d = 1
    while d < seg_len:
        x = op(x, jnp.where(seg_off >= d, pltpu.roll(x, d, 1), fill))
        d *= 2
    return x


def _log_sigmoid(x):
    return -(jnp.maximum(-x, 0.0) + jnp.log1p(jnp.exp(-jnp.abs(x))))


def _split3(x):
    hi = x.astype(BF16).astype(F32)
    r = x - hi
    mid = r.astype(BF16).astype(F32)
    lo = (r - mid).astype(BF16).astype(F32)
    return hi, mid, lo


GP_W_INTER, GP_FLOOR, GP_W, GP_DECAY, GP_A, GP_NEG_U, GP_COUNT = 0, 1, 2, 3, 4, 7, 10


def _mlstm_gates_kernel(gate_ref, bias_ref, out_ref):
    rows, s = gate_ref.shape
    nseq = rows // SUBLANES
    lc = MLSTM_CHUNK
    nh = N_MLSTM_HEADS
    g = gate_ref[...] + bias_ref[...]
    logf = _log_sigmoid(pltpu.roll(g, rows - nh, 0))
    seg = lax.broadcasted_iota(jnp.int32, (1, s), 1) % lc
    bcum = _seg_scan(logf, seg, lc, lambda x, y: x + y, 0.0)
    a = g - bcum
    cmax = _seg_scan(a, seg, lc, jnp.maximum, -jnp.inf)
    m_prev = jnp.zeros((rows, 1), F32)
    for c in range(s // lc):
        blk = slice(c * lc, (c + 1) * lc)
        a_c, b_c = a[:, blk], bcum[:, blk]
        g_c = b_c[:, lc - 1:lc]
        m_cur = jnp.maximum(g_c + m_prev, g_c + jnp.max(a_c, axis=1, keepdims=True))
        u = jnp.maximum(m_prev, cmax[:, blk])
        planes = [jnp.exp(m_prev - u),
                  jnp.exp(-(u + b_c)),
                  jnp.exp(g_c + a_c - m_cur),
                  jnp.broadcast_to(jnp.exp(g_c + m_prev - m_cur), (rows, lc))]
        planes += list(_split3(a_c * LOG2E)) + list(_split3(-u * LOG2E))
        for k, pv in enumerate(planes):
            for bi in range(nseq):
                out_ref[bi, c, k * SUBLANES:(k + 1) * SUBLANES, :] = (
                    pv[bi * SUBLANES:(bi + 1) * SUBLANES, :])
        m_prev = m_cur


def _mlstm_gates(gates2, bias2, *, batch, seq):
    lc = MLSTM_CHUNK
    gb = MLSTM_GATE_SEQS
    return pl.pallas_call(
        _mlstm_gates_kernel,
        grid=(batch // gb,),
        in_specs=[
            pl.BlockSpec((gb * SUBLANES, seq), lambda i: (i, 0)),
            pl.BlockSpec((gb * SUBLANES, 1), lambda i: (0, 0)),
        ],
        out_specs=pl.BlockSpec((gb, seq // lc, GP_COUNT * SUBLANES, lc),
                               lambda i: (i, 0, 0, 0)),
        out_shape=jax.ShapeDtypeStruct((batch, seq // lc, GP_COUNT * SUBLANES, lc), F32),
        compiler_params=_params(("arbitrary",)),
        name="mlstm_gates",
    )(gates2, bias2)


def _mlstm_kernel(gp_ref, qm_ref, kq_ref, vt_ref, og_ref, gn_ref, out_ref, ct_ref):
    s = qm_ref.shape[1]
    lc = MLSTM_CHUNK
    nh = N_MLSTM_HEADS
    ct_ref[...] = jnp.zeros(ct_ref.shape, F32)
    ones_rows = jnp.ones((MLSTM_ONES_ROWS, lc), BF16)
    causal = (lax.broadcasted_iota(jnp.int32, (lc, lc), 0)
              <= lax.broadcasted_iota(jnp.int32, (lc, lc), 1))
    nt = (((1,), (1,)), ((), ()))
    tn = (((0,), (0,)), ((), ()))
    gnb = [jnp.broadcast_to(gn_ref[hd * LANES:(hd + 1) * LANES, :], (LANES, lc))
           for hd in range(nh)]
    row = lax.broadcasted_iota(jnp.int32, (SUBLANES, lc), 0)
    pick = [jnp.where(row == hd, 1.0, 0.0) for hd in range(nh)]

    def plane(gp, k, n=1):
        return gp[k * SUBLANES:(k + n) * SUBLANES]

    def intra(c, hd):
        gp = gp_ref[0, c]
        sl = slice(hd * LANES, (hd + 1) * LANES)
        qm = qm_ref[0, c * lc:(c + 1) * lc, sl]
        kq = kq_ref[0, c * lc:(c + 1) * lc, sl]
        e = pick[hd]
        lhs = jnp.concatenate([plane(gp, GP_A, 3), e, e, e], axis=0).astype(BF16)
        rhs = jnp.concatenate([e, e, e, plane(gp, GP_NEG_U, 3)], axis=0).astype(BF16)
        arg = lax.dot_general(lhs, rhs, tn, preferred_element_type=F32)
        st = lax.dot_general(kq, qm, nt, preferred_element_type=F32)
        return (st * jnp.exp2(jnp.where(causal, arg, -jnp.inf))).astype(BF16)

    units = [(c, hd) for c in range(s // lc) for hd in range(nh)]
    ahead = [intra(*un) for un in units[:MLSTM_SCORES_AHEAD]]
    for ui, (c, hd) in enumerate(units):
        pt = ahead.pop(0)
        if ui + MLSTM_SCORES_AHEAD < len(units):
            ahead.append(intra(*units[ui + MLSTM_SCORES_AHEAD]))
        start = c * lc
        gp = gp_ref[0, c]
        sl = slice(hd * LANES, (hd + 1) * LANES)
        qm = qm_ref[0, pl.ds(start, lc), sl]
        kq = kq_ref[0, pl.ds(start, lc), sl]
        vt = jnp.concatenate([vt_ref[0, c, sl, :], ones_rows], axis=0)
        w_inter = plane(gp, GP_W_INTER)[hd:hd + 1]
        floor = plane(gp, GP_FLOOR)[hd:hd + 1]
        w_row = plane(gp, GP_W)[hd:hd + 1]
        decay = plane(gp, GP_DECAY)[hd:hd + 1, 0:1]
        ct = ct_ref[hd]
        nd = (jnp.dot(vt, pt, preferred_element_type=F32)
              + w_inter * lax.dot_general(ct.astype(BF16), qm, nt,
                                          preferred_element_type=F32))
        den = jnp.maximum(jnp.abs(nd[LANES:LANES + 1]), floor)
        ht = nd[0:LANES] / den
        ms = jnp.mean(ht * ht, axis=0, keepdims=True)
        hn = ht * lax.rsqrt(ms + EPS) * gnb[hd]
        og = og_ref[0, pl.ds(start, lc), sl].astype(F32)
        out_ref[0, pl.ds(start, lc), sl] = (hn.T * og).astype(out_ref.dtype)
        vw = (vt.astype(F32) * w_row).astype(BF16)
        ct_ref[hd] = decay * ct + jnp.dot(vw, kq, preferred_element_type=F32)


def _mlstm(gplanes, qm, kq, mvt, og, gn_col):
    b, s, w = qm.shape
    lc = MLSTM_CHUNK
    seq = lambda bi: (bi, 0, 0)
    seq4 = lambda bi: (bi, 0, 0, 0)
    return pl.pallas_call(
        _mlstm_kernel,
        grid=(b,),
        in_specs=[
            pl.BlockSpec((1,) + gplanes.shape[1:], seq4),
            pl.BlockSpec((1, s, w), seq),
            pl.BlockSpec((1, s, w), seq),
            pl.BlockSpec((1, s // lc, w, lc), seq4),
            pl.BlockSpec((1, s, w), seq),
            pl.BlockSpec((w, 1), lambda bi: (0, 0)),
        ],
        out_specs=pl.BlockSpec((1, s, w), seq),
        out_shape=jax.ShapeDtypeStruct((b, s, w), BF16),
        scratch_shapes=[
            pltpu.VMEM((N_MLSTM_HEADS, LANES + MLSTM_ONES_ROWS, LANES), F32),
        ],
        compiler_params=_params(("arbitrary",)),
        name="mlstm",
    )(gplanes, qm, kq, mvt, og, gn_col)


def _out_mlp_kernel(x_ref, od_ref, om_ref, wo_ref, gpost_ref, gpre_ref,
                    wu_ref, wdn_ref, gmlp_ref, o_ref):
    half = x_ref.shape[0] // 2
    rows = (slice(0, half), slice(half, 2 * half))
    nd = od_ref.shape[1]

    def mix(rs):
        mixed = (jnp.dot(od_ref[rs, :], wo_ref[0:nd, :], preferred_element_type=F32)
                 + jnp.dot(om_ref[rs, :], wo_ref[nd:, :], preferred_element_type=F32))
        x1 = x_ref[rs, :] + _rms(mixed, gpost_ref[...])
        return x1, _rms(x1, gpre_ref[...]).astype(BF16)

    def mlp(h):
        acc = None
        for f in range(wu_ref.shape[1] // MLP_FF_CHUNK):
            sl = slice(f * MLP_FF_CHUNK, (f + 1) * MLP_FF_CHUNK)
            a = jnp.maximum(jnp.dot(h, wu_ref[:, sl], preferred_element_type=F32), 0.0)
            part = jnp.dot((a * a).astype(BF16), wdn_ref[sl, :], preferred_element_type=F32)
            acc = part if acc is None else acc + part
        return acc

    xa, ha = mix(rows[0])
    xb, hb = mix(rows[1])
    acc_a = mlp(ha)
    acc_b = mlp(hb)
    o_ref[rows[0], :] = xa + _rms(acc_a, gmlp_ref[...])
    o_ref[rows[1], :] = xb + _rms(acc_b, gmlp_ref[...])


def _out_mlp(x2, od, om, wo, gpost, gpre, wu, wdn, gmlp):
    t, d = x2.shape
    tm = MLP_ROWS
    row = lambda i: (i, 0)
    const = lambda i: (0, 0)
    resident = lambda shape: pl.BlockSpec(shape, const, pipeline_mode=pl.Buffered(1))
    return pl.pallas_call(
        _out_mlp_kernel,
        grid=(t // tm,),
        in_specs=[
            pl.BlockSpec((tm, d), row),
            pl.BlockSpec((tm, od.shape[1]), row),
            pl.BlockSpec((tm, om.shape[1]), row),
            resident(wo.shape),
            pl.BlockSpec((1, d), const),
            pl.BlockSpec((1, d), const),
            resident(wu.shape),
            resident(wdn.shape),
            pl.BlockSpec((1, d), const),
        ],
        out_specs=pl.BlockSpec((tm, d), row),
        out_shape=jax.ShapeDtypeStruct((t, d), F32),
        compiler_params=_params(("arbitrary",)),
        name="out_mlp",
    )(x2, od, om, wo, gpost, gpre, wu, wdn, gmlp)


def _rope_tables(seq):
    d = DIFF_HEAD_DIM
    inv = ROPE_THETA ** (-jnp.arange(0, d, 2, dtype=F32) / d)
    ang = jnp.arange(seq, dtype=jnp.int32).astype(F32)[:, None] * inv[None, :]
    cos = jnp.cos(ang)
    sin = jnp.sin(ang)
    zero = jnp.zeros_like(sin)
    reps = LANES // d
    cos_t = jnp.tile(jnp.concatenate([cos, cos], axis=1), (1, reps))
    slo_t = jnp.tile(jnp.concatenate([-sin, zero], axis=1), (1, reps))
    shi_t = jnp.tile(jnp.concatenate([zero, sin], axis=1), (1, reps))
    return cos_t, slo_t, shi_t


def _layer(x, l, norm_mix_pre, w_in, conv_w, conv_b, b_igate, b_fgate, lambda_q1,
           lambda_k1, lambda_q2, lambda_k2, diff_norm, mlstm_norm, w_out,
           norm_mix_post, norm_mlp_pre, w_up, w_down, norm_mlp_post):
    b, s, d = x.shape
    nh, dqk = N_MLSTM_HEADS, MLSTM_QK_DIM
    x2 = x.reshape(b * s, d)

    def qk_interleave(v):
        lead = v.shape[:-1]
        q = v[..., :nh * dqk].reshape(lead + (nh, dqk))
        k = v[..., nh * dqk:].reshape(lead + (nh, dqk))
        return jnp.concatenate([q, k], axis=-1).reshape(lead + (2 * nh * dqk,))

    w = w_in[l]
    m0, m1, g1 = COLS_MQK[0], COLS_MQK[1], COLS_MO[1] + 2 * nh
    wg = jnp.pad(w[:, COLS_MO[1]:g1], ((0, 0), (0, COLS_GATES[1] - g1)))
    w_main = jnp.concatenate([w[:, :m0], qk_interleave(w[:, m0:m1]), w[:, m1:COLS_MO[1]], wg],
                             axis=1).astype(BF16)

    cw = qk_interleave(conv_w[l])
    cb = qk_interleave(conv_b[l])[None, :]
    qs = qk_interleave(jnp.concatenate([jnp.full((nh * dqk,), dqk ** -0.5, F32),
                                        jnp.zeros((nh * dqk,), F32)]))[None, :]
    cos_t, slo_t, shi_t = _rope_tables(s)

    dq, dk, dv, qm, kq, mvt, og, gates, wo, wu, wdn = _in_proj(
        x2, norm_mix_pre[l][None, :], w_main, cos_t, slo_t, shi_t, cw, cb, qs,
        (w_out[l], w_up[l], w_down[l]), batch=b, seq=s)

    lam_p = jnp.stack([lambda_q1[l], lambda_k1[l], lambda_q2[l], lambda_k2[l]]).astype(F32)
    o_diff = _diff_attn(lam_p, diff_norm[l][:, None], dq.reshape(b, s, -1),
                        dk.reshape(b, s, -1), dv.reshape(b, s, -1),
                        lam_init=_lambda_init(l))

    bias = jnp.concatenate([b_igate[l], b_fgate[l]]).astype(F32)[:, None]
    gplanes = _mlstm_gates(gates.reshape(b * SUBLANES, s),
                           jnp.tile(bias, (MLSTM_GATE_SEQS, 1)), batch=b, seq=s)
    o_mlstm = _mlstm(gplanes, qm.reshape(b, s, -1), kq.reshape(b, s, -1), mvt,
                     og.reshape(b, s, -1), mlstm_norm[l].reshape(-1, 1))

    out = _out_mlp(x2, o_diff.reshape(b * s, -1), o_mlstm.reshape(b * s, -1),
                   wo, norm_mix_post[l][None, :], norm_mlp_pre[l][None, :],
                   wu, wdn, norm_mlp_post[l][None, :])
    return out.reshape(b, s, d)


def kernel(x, norm_mix_pre, w_in, conv_w, conv_b, b_igate, b_fgate, lambda_q1, lambda_k1,
           lambda_q2, lambda_k2, diff_norm, mlstm_norm, w_out, norm_mix_post,
           norm_mlp_pre, w_up, w_down, norm_mlp_post):
    for l in range(w_in.shape[0]):
        x = _layer(x, l, norm_mix_pre, w_in, conv_w, conv_b, b_igate, b_fgate,
                   lambda_q1, lambda_k1, lambda_q2, lambda_k2, diff_norm, mlstm_norm,
                   w_out, norm_mix_post, norm_mlp_pre, w_up, w_down, norm_mlp_post)
    return x
```
